```python
import math
import jax, jax.numpy as jnp
from jax import lax
import numpy as np

D_MODEL = 2048
BATCH = 8
SEQ = 2048
DEPTH = 1
DEC_BATCH = 2
DEC_SEQ = 16384
PAST_LEN = 128

SSM_EXPAND = 2
D_INNER = SSM_EXPAND * D_MODEL
SSM_HEAD_DIM = 64
SSM_HEADS = D_INNER // SSM_HEAD_DIM
SSM_GROUPS = 8
D_STATE = 128
GN = SSM_GROUPS * D_STATE
CONV_DIM = D_INNER + 2 * GN
CONV_WIDTH = 5
SSD_CHUNK = 128
ATTN_HEAD_DIM = 128
ATTN_HEADS = D_MODEL // ATTN_HEAD_DIM
KV_HEADS = 4
Q_PER_KV = ATTN_HEADS // KV_HEADS
WINDOW = 128
ATTN_BLOCK = 128
ATTN_SCALE = ATTN_HEAD_DIM ** -0.5
REL_BUCKETS = 32
REL_MAX_DIST = 128
D_FF = -(-8 * D_MODEL // (3 * 256)) * 256
EPS = 1e-6
Q_DIM = ATTN_HEADS * ATTN_HEAD_DIM
KV_DIM = KV_HEADS * ATTN_HEAD_DIM
IN_SPLITS = (D_INNER, CONV_DIM, 2 * SSM_HEADS, Q_DIM, KV_DIM, KV_DIM, 2 * D_MODEL)
IN_PROJ_DIM = D_INNER + CONV_DIM + 2 * SSM_HEADS + Q_DIM + 2 * KV_DIM + 2 * D_MODEL

kernel_name = 'hybrid_ssd_swa_encoder'


def rmsnorm(x, w):
    xf = x.astype(jnp.float32)
    y = xf * lax.rsqrt(jnp.mean(xf * xf, axis=-1, keepdims=True) + EPS) * w.astype(jnp.float32)
    return y.astype(x.dtype)


def centred_depthwise_conv(x, w, b):
    S = x.shape[1]
    half = CONV_WIDTH // 2
    xp = jnp.pad(x, ((0, 0), (half, half), (0, 0)))
    out = b
    for j in range(CONV_WIDTH):
        out = out + xp[:, j:j + S] * w[j]
    return out


def ssd_chunked_scan(x, dt, a, bm, cm):
    bsz, S, H, P = x.shape
    nc = S // SSD_CHUNK
    K = H // SSM_GROUPS

    def chunks(t):
        t = t.reshape((bsz, nc, SSD_CHUNK) + t.shape[2:])
        return jnp.moveaxis(t, 1, 0)

    xc = chunks(x.reshape(bsz, S, SSM_GROUPS, K, P))
    dtc = chunks(dt.reshape(bsz, S, SSM_GROUPS, K))
    ac = chunks((dt * a).reshape(bsz, S, SSM_GROUPS, K))
    bc, cc = chunks(bm), chunks(cm)
    lower = jnp.tril(jnp.ones((SSD_CHUNK, SSD_CHUNK), dtype=bool))

    def step(state, inp):
        xk, dtk, ak, bk, ck = inp
        acum = jnp.cumsum(ak, axis=1)
        xdt = xk * dtk[..., None]
        at = jnp.moveaxis(acum, 1, -1)
        seg = at[..., :, None] - at[..., None, :]
        decay = jnp.exp(jnp.where(lower, seg, -jnp.inf))
        cb = jnp.einsum('blgn,bsgn->bgls', ck, bk)
        y = jnp.einsum('bgls,bgkls,bsgkp->blgkp', cb, decay, xdt)
        y = y + jnp.einsum('blgn,bgkpn,blgk->blgkp', ck, state, jnp.exp(acum))
        last = acum[:, -1]
        w_in = jnp.exp(last[:, None] - acum)
        state = state * jnp.exp(last)[..., None, None] + jnp.einsum('bsgn,bsgk,bsgkp->bgkpn', bk, w_in, xdt)
        return state, y

    state0 = jnp.zeros((bsz, SSM_GROUPS, K, P, D_STATE), jnp.float32)
    _, y = lax.scan(step, state0, (xc, dtc, ac, bc, cc))
    return jnp.moveaxis(y, 0, 1).reshape(bsz, S, H, P)


def ssd_mixer(z, xbc, dt_raw, conv_w, conv_b, dt_bias, a_log, d_skip, norm_w, w_branch):
    bsz, S, _ = z.shape
    f32 = jnp.float32
    xbc = jax.nn.silu(centred_depthwise_conv(xbc, conv_w, conv_b)).astype(f32)
    xs = xbc[..., :D_INNER].reshape(bsz, S, SSM_HEADS, SSM_HEAD_DIM)
    bm = xbc[..., D_INNER:D_INNER + GN].reshape(bsz, S, SSM_GROUPS, D_STATE)
    cm = xbc[..., D_INNER + GN:].reshape(bsz, S, SSM_GROUPS, D_STATE)
    dt = jax.nn.softplus(dt_raw.astype(f32).reshape(bsz, S, 2, SSM_HEADS) + dt_bias.astype(f32))
    a = -jnp.exp(a_log.astype(f32))
    flip = lambda t: jnp.flip(t, axis=1)
    y_fwd = ssd_chunked_scan(xs, dt[:, :, 0], a[0], bm, cm)
    y_bwd = flip(ssd_chunked_scan(flip(xs), flip(dt[:, :, 1]), a[1], flip(bm), flip(cm)))
    y = y_fwd + y_bwd + xs * d_skip.astype(f32)[:, None]
    y = y.reshape(bsz, S, D_INNER) * jax.nn.silu(z.astype(f32))
    yg = y.reshape(bsz, S, SSM_GROUPS, D_INNER // SSM_GROUPS)
    yg = yg * lax.rsqrt(jnp.mean(yg * yg, axis=-1, keepdims=True) + EPS)
    y = yg.reshape(bsz, S, D_INNER) * norm_w.astype(f32)
    return y.astype(z.dtype) @ w_branch


def t5_buckets(rel):
    half = REL_BUCKETS // 2
    ret = (rel > 0).astype(np.int32) * half
    n = np.abs(rel)
    max_exact = half // 2
    large = max_exact + (np.log(np.maximum(n, 1) / max_exact) / np.log(REL_MAX_DIST / max_exact)
                         * (half - max_exact)).astype(np.int32)
    large = np.minimum(large, half - 1)
    return ret + np.where(n < max_exact, n, large).astype(np.int32)


def windowed_gqa(q, k, v, rel_bias, sink, w_branch):
    bsz, S, _ = q.shape
    nb = S // ATTN_BLOCK
    blk = ATTN_BLOCK
    q = q.reshape(bsz, nb, blk, KV_HEADS, Q_PER_KV, ATTN_HEAD_DIM)

    def band(t):
        t = t.reshape(bsz, S, KV_HEADS, ATTN_HEAD_DIM)
        t = jnp.pad(t, ((0, 0), (blk, blk), (0, 0), (0, 0))).reshape(bsz, nb + 2, blk, KV_HEADS, ATTN_HEAD_DIM)
        return jnp.concatenate([t[:, :-2], t[:, 1:-1], t[:, 2:]], axis=2)

    kw, vw = band(k), band(v)
    logits = jnp.einsum('bnqhrd,bnkhd->bnhrqk', q, kw).astype(jnp.float32) * ATTN_SCALE
    rel = np.arange(3 * blk)[None, :] - blk - np.arange(blk)[:, None]
    bias = rel_bias.astype(jnp.float32)[t5_buckets(rel)]
    bias = jnp.transpose(bias, (2, 0, 1)).reshape(KV_HEADS, Q_PER_KV, blk, 3 * blk)
    key_pos = (np.arange(nb)[:, None] - 1) * blk + np.arange(3 * blk)[None, :]
    valid = (np.abs(rel) <= WINDOW)[None] & ((key_pos >= 0) & (key_pos < S))[:, None, :]
    logits = jnp.where(valid[None, :, None, None], logits + bias[None, None], -jnp.inf)
    sink_l = sink.astype(jnp.float32).reshape(KV_HEADS, Q_PER_KV)[None, None, :, :, None, None]
    m = jnp.maximum(jnp.max(logits, axis=-1, keepdims=True), sink_l)
    p = jnp.exp(logits - m)
    probs = p / (jnp.sum(p, axis=-1, keepdims=True) + jnp.exp(sink_l - m))
    out = jnp.einsum('bnhrqk,bnkhd->bnqhrd', probs.astype(v.dtype), vw)
    return out.reshape(bsz, S, Q_DIM) @ w_branch


def trunk(x, mix_norm_w, w_in, conv_w, conv_b, dt_bias, a_log, d_skip, ssm_norm_w, w_ssm_branch,
          rel_bias, attn_sink, w_attn_branch, w_out, ffn_norm_w, w_ffn_in, w_ffn_out, final_norm_w):
    bsz, S, _ = x.shape
    h = x
    cuts = list(np.cumsum(IN_SPLITS)[:-1])
    for l in range(DEPTH):
        xn = rmsnorm(h, mix_norm_w[l])
        proj = xn @ w_in[l]
        z, xbc, dt_raw, q, k, v, gate_pre = jnp.split(proj, cuts, axis=-1)
        y_a = ssd_mixer(z, xbc, dt_raw, conv_w[l], conv_b[l], dt_bias[l], a_log[l], d_skip[l],
                        ssm_norm_w[l], w_ssm_branch[l])
        y_b = windowed_gqa(q, k, v, rel_bias, attn_sink[l], w_attn_branch[l])
        g = jax.nn.sigmoid(gate_pre.astype(jnp.float32)).reshape(bsz, S, 2, D_MODEL)
        merged = (g[:, :, 0] * y_a.astype(jnp.float32) + g[:, :, 1] * y_b.astype(jnp.float32)).astype(x.dtype)
        h = h + merged @ w_out[l]
        hn = rmsnorm(h, ffn_norm_w[l])
        gt, up = jnp.split(hn @ w_ffn_in[l], 2, axis=-1)
        h = h + (jax.nn.silu(gt) * up) @ w_ffn_out[l]
    return rmsnorm(h, final_norm_w)


def setup_inputs(seed: int = 0) -> dict:
    key = jax.random.key(seed)
    ks = jax.random.split(key, 20)
    f32 = jnp.float32
    nrm = lambda k, shape, s: jax.random.normal(k, shape, f32) * s
    u = jax.random.uniform(ks[6], (DEPTH, 2, SSM_HEADS), f32)
    dt0 = jnp.exp(u * (math.log(0.1) - math.log(0.001)) + math.log(0.001))
    dt_bias = dt0 + jnp.log(-jnp.expm1(-dt0))
    return {
        'x_prompt': nrm(ks[0], (BATCH, SEQ, D_MODEL), 1.0),
        'x_sample': nrm(ks[1], (DEC_BATCH, DEC_SEQ, D_MODEL), 1.0),
        'mix_norm_w': 1.0 + nrm(ks[2], (DEPTH, D_MODEL), 0.02),
        'w_in': nrm(ks[3], (DEPTH, D_MODEL, IN_PROJ_DIM), D_MODEL ** -0.5),
        'conv_w': nrm(ks[4], (DEPTH, CONV_WIDTH, CONV_DIM), CONV_WIDTH ** -0.5),
        'conv_b': nrm(ks[5], (DEPTH, CONV_DIM), 0.01),
        'dt_bias': dt_bias,
        'a_log': jnp.log(jax.random.uniform(ks[7], (DEPTH, 2, SSM_HEADS), f32, 1.0, 16.0)),
        'd_skip': 1.0 + nrm(ks[8], (DEPTH, SSM_HEADS), 0.02),
        'ssm_norm_w': 1.0 + nrm(ks[9], (DEPTH, D_INNER), 0.02),
        'w_ssm_branch': nrm(ks[10], (DEPTH, D_INNER, D_MODEL), D_INNER ** -0.5),
        'rel_bias': nrm(ks[11], (REL_BUCKETS, ATTN_HEADS), 0.5),
        'attn_sink': nrm(ks[12], (DEPTH, ATTN_HEADS), 0.5),
        'w_attn_branch': nrm(ks[13], (DEPTH, Q_DIM, D_MODEL), Q_DIM ** -0.5),
        'w_out': nrm(ks[14], (DEPTH, D_MODEL, D_MODEL), D_MODEL ** -0.5),
        'ffn_norm_w': 1.0 + nrm(ks[15], (DEPTH, D_MODEL), 0.02),
        'w_ffn_in': nrm(ks[16], (DEPTH, D_MODEL, 2 * D_FF), D_MODEL ** -0.5),
        'w_ffn_out': nrm(ks[17], (DEPTH, D_FF, D_MODEL), D_FF ** -0.5),
        'final_norm_w': 1.0 + nrm(ks[18], (D_MODEL,), 0.02),
    }


def reference(x_prompt, x_sample, mix_norm_w, w_in, conv_w, conv_b, dt_bias, a_log, d_skip, ssm_norm_w,
              w_ssm_branch, rel_bias, attn_sink, w_attn_branch, w_out, ffn_norm_w, w_ffn_in, w_ffn_out,
              final_norm_w):
    y_prompt = trunk(x_prompt, mix_norm_w, w_in, conv_w, conv_b, dt_bias, a_log, d_skip, ssm_norm_w,
                     w_ssm_branch, rel_bias, attn_sink, w_attn_branch, w_out, ffn_norm_w, w_ffn_in,
                     w_ffn_out, final_norm_w)
    y_sample = trunk(x_sample, mix_norm_w, w_in, conv_w, conv_b, dt_bias, a_log, d_skip, ssm_norm_w,
                     w_ssm_branch, rel_bias, attn_sink, w_attn_branch, w_out, ffn_norm_w, w_ffn_in,
                     w_ffn_out, final_norm_w)
    return (y_prompt, y_sample)
```

```python
import functools
import math

import numpy as np
import jax
import jax.numpy as jnp
from jax import lax
from jax.experimental import pallas as pl
from jax.experimental.pallas import tpu as pltpu

F32 = jnp.float32
BF16 = jnp.bfloat16

EPS = 1e-6
HEAD_DIM_SSM = 64
STATE_DIM = 128
SSM_GROUPS = 8
CHUNK = 128
CONV_WIDTH = 5
ATTN_HEAD_DIM = 128
KV_HEADS = 4
Q_PER_KV = 4
WINDOW = 128
ATTN_BLOCK = 128
REL_BUCKETS = 32
REL_MAX_DIST = 128
COL_BLOCK = 512
HALO_ROWS = 16
MASK_VALUE = -1e30
V7X_VMEM_BYTES = 64 * 1024 * 1024
VMEM_CAP = 60000 * 1024


def _vmem_limit(estimate_bytes):
    return int(min(VMEM_CAP, max(estimate_bytes * 5 // 4, 16 * 1024 * 1024)))


def _params(semantics, vmem_estimate):
    return pltpu.CompilerParams(dimension_semantics=semantics,
                                vmem_limit_bytes=_vmem_limit(vmem_estimate))


def _sigmoid(x):
    return 1.0 / (1.0 + jnp.exp(-x))


def _silu(x):
    return x * _sigmoid(x)


def _softplus(x):
    return jnp.maximum(x, 0.0) + jnp.log1p(jnp.exp(-jnp.abs(x)))


def _rmsnorm(x, w):
    ms = jnp.mean(x * x, axis=-1, keepdims=True)
    return x * lax.rsqrt(ms + EPS) * w


def _tile(n, want):
    t = min(n, want)
    while n % t:
        t //= 2
    return t


def _inproj_kernel(x_ref, nw_ref, w_ref, wdt_ref, o_ref, dt_ref, xn_ref):
    @pl.when(pl.program_id(1) == 0)
    def _():
        xn = _rmsnorm(x_ref[...], nw_ref[...]).astype(BF16)
        xn_ref[...] = xn
        dt_ref[...] = jnp.dot(xn, wdt_ref[...], preferred_element_type=F32)

    o_ref[...] = jnp.dot(xn_ref[...], w_ref[...], preferred_element_type=F32).astype(o_ref.dtype)


def _in_proj(x, norm_w, w_main, w_dt):
    t, d = x.shape
    n = w_main.shape[1]
    nb = n // COL_BLOCK
    tm = _tile(t, 1024)
    ndt = w_dt.shape[1]
    est = 2 * tm * d * 4 + tm * d * 2 + 2 * d * COL_BLOCK * 2 + 2 * tm * COL_BLOCK * 2 + 2 * d * ndt * 2 + 2 * tm * ndt * 4
    return pl.pallas_call(
        _inproj_kernel,
        grid=(t // tm, nb),
        in_specs=[
            pl.BlockSpec((tm, d), lambda i, j: (i, 0)),
            pl.BlockSpec((1, d), lambda i, j: (0, 0)),
            pl.BlockSpec((d, COL_BLOCK), lambda i, j: (0, j)),
            pl.BlockSpec((d, ndt), lambda i, j: (0, 0)),
        ],
        out_specs=[
            pl.BlockSpec((None, tm, COL_BLOCK), lambda i, j: (j, i, 0)),
            pl.BlockSpec((tm, ndt), lambda i, j: (i, 0)),
        ],
        out_shape=[
            jax.ShapeDtypeStruct((nb, t, COL_BLOCK), BF16),
            jax.ShapeDtypeStruct((t, ndt), F32),
        ],
        scratch_shapes=[pltpu.VMEM((tm, d), BF16)],
        compiler_params=_params(("parallel", "arbitrary"), est),
        name="in_proj",
    )(x, norm_w.reshape(1, d), w_main, w_dt)


def _conv_kernel(main_ref, prev_ref, next_ref, w_ref, b_ref, xs_ref, bm_ref, cm_ref, win_ref, *, tr, n_xs):
    i = pl.program_id(1)
    has_prev = (i > 0).astype(F32)
    has_next = (i < pl.num_programs(1) - 1).astype(F32)
    half = CONV_WIDTH // 2
    n_blocks = main_ref.shape[0]

    def conv_block(c):
        win_ref[0:HALO_ROWS, :] = prev_ref[c].astype(F32) * has_prev
        win_ref[HALO_ROWS:HALO_ROWS + tr, :] = main_ref[c].astype(F32)
        win_ref[HALO_ROWS + tr:, :] = next_ref[c].astype(F32) * has_next
        acc = jnp.broadcast_to(b_ref[c], (tr, COL_BLOCK))
        for j in range(CONV_WIDTH):
            start = HALO_ROWS - half + j
            acc = acc + win_ref[start:start + tr, :] * w_ref[c, j:j + 1, :]
        return _silu(acc)

    def xs_body(c, carry):
        xs_ref[c] = conv_block(c).astype(xs_ref.dtype)
        return carry

    lax.fori_loop(0, n_xs, xs_body, 0)
    per = COL_BLOCK // STATE_DIM
    for c in range(n_xs, n_blocks):
        act = conv_block(c).astype(bm_ref.dtype)
        dst = bm_ref if c < n_xs + (n_blocks - n_xs) // 2 else cm_ref
        base = (c - n_xs) % ((n_blocks - n_xs) // 2) * per
        for k in range(per):
            dst[base + k] = act[:, k * STATE_DIM:(k + 1) * STATE_DIM]


def _conv_silu(proj, conv_w, conv_b, bsz, seq, first_block, n_xs):
    t = proj.shape[1]
    n_blocks = conv_w.shape[0]
    tr = _tile(seq, 512)
    nr = seq // tr
    hb = tr // HALO_ROWS
    last_halo = t // HALO_ROWS - 1
    n_bc = (n_blocks - n_xs) // 2 * (COL_BLOCK // STATE_DIM)
    cb0 = first_block // n_blocks
    assert cb0 * n_blocks == first_block
    est = (2 * n_blocks * (tr + 2 * HALO_ROWS) * COL_BLOCK * 2 + 2 * (n_xs * tr * COL_BLOCK + 2 * n_bc * tr * STATE_DIM) * 2
           + (tr + 2 * HALO_ROWS) * COL_BLOCK * 4 + 4 * n_blocks * 8 * COL_BLOCK * 4)
    kern = functools.partial(_conv_kernel, tr=tr, n_xs=n_xs)
    return pl.pallas_call(
        kern,
        grid=(bsz, nr),
        in_specs=[
            pl.BlockSpec((n_blocks, tr, COL_BLOCK), lambda b, i: (cb0, b * nr + i, 0)),
            pl.BlockSpec((n_blocks, HALO_ROWS, COL_BLOCK),
                         lambda b, i: (cb0, jnp.maximum((b * nr + i) * hb - 1, 0), 0)),
            pl.BlockSpec((n_blocks, HALO_ROWS, COL_BLOCK),
                         lambda b, i: (cb0, jnp.minimum((b * nr + i + 1) * hb, last_halo), 0)),
            pl.BlockSpec((n_blocks, CONV_WIDTH, COL_BLOCK), lambda b, i: (0, 0, 0)),
            pl.BlockSpec((n_blocks, 1, COL_BLOCK), lambda b, i: (0, 0, 0)),
        ],
        out_specs=[
            pl.BlockSpec((n_xs, tr, COL_BLOCK), lambda b, i: (0, b * nr + i, 0)),
            pl.BlockSpec((n_bc, tr, STATE_DIM), lambda b, i: (0, b * nr + i, 0)),
            pl.BlockSpec((n_bc, tr, STATE_DIM), lambda b, i: (0, b * nr + i, 0)),
        ],
        out_shape=[
            jax.ShapeDtypeStruct((n_xs, t, COL_BLOCK), BF16),
            jax.ShapeDtypeStruct((n_bc, t, STATE_DIM), BF16),
            jax.ShapeDtypeStruct((n_bc, t, STATE_DIM), BF16),
        ],
        scratch_shapes=[pltpu.VMEM((tr + 2 * HALO_ROWS, COL_BLOCK), F32)],
        compiler_params=_params(("parallel", "parallel"), est),
        name="conv_silu",
    )(proj, proj, proj, conv_w, conv_b)


def _scan_rows(v, reverse):
    n = v.shape[0]
    row = lax.broadcasted_iota(jnp.int32, v.shape, 0)
    k = 1
    while k < n:
        if reverse:
            shifted = pltpu.roll(v, n - k, axis=0)
            keep = row < n - k
        else:
            shifted = pltpu.roll(v, k, axis=0)
            keep = row >= k
        v = v + jnp.where(keep, shifted, 0.0)
        k *= 2
    return v


def _ssd_chunk_prologue(dt_ref, dtb_ref, a_ref, a_sc, at_sc, dtt_sc, wdt_sc, reverse):
    dt = _softplus(dt_ref[...] + dtb_ref[...])
    acc = _scan_rows(dt * a_ref[...], reverse)
    a_sc[...] = acc
    acc_t = acc.T
    dt_t = dt.T
    at_sc[...] = acc_t
    dtt_sc[...] = dt_t
    n = acc.shape[0]
    total = acc_t[:, 0:1] if reverse else acc_t[:, n - 1:n]
    wdt_sc[...] = jnp.exp(total - acc_t) * dt_t


def _ssd_group(g, xs_ref, bm_ref, cm_ref, st_ref, a_sc, at_sc, dtt_sc, wdt_sc, head0, reverse):
    l = CHUNK
    xs = xs_ref[g]
    bm = bm_ref[g]
    cm = cm_ref[g]
    cb = lax.dot_general(cm, bm, (((1,), (1,)), ((), ())), preferred_element_type=F32)
    bm_t = bm.astype(F32).T
    st = st_ref[g]
    y_off = jnp.dot(cm, st.astype(BF16), preferred_element_type=F32)

    heads_per_group = COL_BLOCK // HEAD_DIM_SSM
    first = head0 + g * heads_per_group
    a_cols = pltpu.roll(a_sc[...], (2 * l - first) % l, axis=1)
    row_i = lax.broadcasted_iota(jnp.int32, (l, l), 0)
    col_i = lax.broadcasted_iota(jnp.int32, (l, l), 1)
    causal = (row_i <= col_i) if reverse else (row_i >= col_i)
    lane = lax.broadcasted_iota(jnp.int32, (l, 2 * HEAD_DIM_SSM), 1)
    low_half = lane < HEAD_DIM_SSM
    keep_low = jnp.where(low_half, 1.0, 0.0).astype(BF16)
    keep_high = jnp.where(low_half, 0.0, 1.0).astype(BF16)
    edge = 0 if reverse else l - 1

    y_slabs = []
    st_slabs = []
    for pair in range(heads_per_group // 2):
        sl = slice(pair * 2 * HEAD_DIM_SSM, (pair + 1) * 2 * HEAD_DIM_SSM)
        xs_pair = xs[:, sl]
        rhs = jnp.concatenate([xs_pair * keep_low, xs_pair * keep_high], axis=0)
        m_parts, bw_parts, a_col_parts, a_tot_parts = [], [], [], []
        for sub in range(2):
            k = pair * 2 + sub
            a_col = a_cols[:, k:k + 1]
            a_row = at_sc[pl.ds(first + k, 1), :]
            dt_row = dtt_sc[pl.ds(first + k, 1), :]
            wd_row = wdt_sc[pl.ds(first + k, 1), :]
            seg = jnp.where(causal, a_col - a_row, -jnp.inf)
            m_parts.append((cb * jnp.exp(seg) * dt_row).astype(BF16))
            bw_parts.append((bm_t * wd_row).astype(BF16))
            a_col_parts.append(a_col)
            a_tot_parts.append(a_cols[edge:edge + 1, k:k + 1])
        m_pair = jnp.concatenate(m_parts, axis=1)
        bw_pair = jnp.concatenate(bw_parts, axis=1)
        a_col_pair = jnp.where(low_half, a_col_parts[0], a_col_parts[1])
        a_tot_pair = jnp.where(low_half[0:1], a_tot_parts[0], a_tot_parts[1])
        y_pair = jnp.dot(m_pair, rhs, preferred_element_type=F32) + jnp.exp(a_col_pair) * y_off[:, sl]
        st_pair = jnp.exp(a_tot_pair) * st[:, sl] + jnp.dot(bw_pair, rhs, preferred_element_type=F32)
        y_slabs.append(y_pair)
        st_slabs.append(st_pair)
    st_ref[g] = jnp.concatenate(st_slabs, axis=1)
    return jnp.concatenate(y_slabs, axis=1), xs


def _ssd_fwd_kernel(xs_ref, bm_ref, cm_ref, dt_ref, dtb_ref, a_ref, y_ref, st_ref, a_sc, at_sc, dtt_sc, wdt_sc):
    @pl.when(pl.program_id(1) == 0)
    def _():
        st_ref[...] = jnp.zeros_like(st_ref)

    _ssd_chunk_prologue(dt_ref, dtb_ref, a_ref, a_sc, at_sc, dtt_sc, wdt_sc, reverse=False)

    def body(g, carry):
        y, _ = _ssd_group(g, xs_ref, bm_ref, cm_ref, st_ref, a_sc, at_sc, dtt_sc, wdt_sc, head0=0, reverse=False)
        y_ref[g] = y.astype(y_ref.dtype)
        return carry

    lax.fori_loop(0, SSM_GROUPS, body, 0)


def _ssd_bwd_kernel(xs_ref, bm_ref, cm_ref, dt_ref, dtb_ref, a_ref, yf_ref, z_ref, dskip_ref, nw_ref, y_ref,
                    st_ref, a_sc, at_sc, dtt_sc, wdt_sc, *, n_heads):
    @pl.when(pl.program_id(1) == 0)
    def _():
        st_ref[...] = jnp.zeros_like(st_ref)

    _ssd_chunk_prologue(dt_ref, dtb_ref, a_ref, a_sc, at_sc, dtt_sc, wdt_sc, reverse=True)

    def body(g, carry):
        y, xs = _ssd_group(g, xs_ref, bm_ref, cm_ref, st_ref, a_sc, at_sc, dtt_sc, wdt_sc, head0=n_heads, reverse=True)
        y = y + yf_ref[g].astype(F32) + xs.astype(F32) * dskip_ref[g]
        y = y * _silu(z_ref[g].astype(F32))
        y_ref[g] = _rmsnorm(y, nw_ref[g]).astype(y_ref.dtype)
        return carry

    lax.fori_loop(0, SSM_GROUPS, body, 0)


def _ssd_scratch(two_h):
    return [
        pltpu.VMEM((SSM_GROUPS, STATE_DIM, COL_BLOCK), F32),
        pltpu.VMEM((CHUNK, two_h), F32),
        pltpu.VMEM((two_h, CHUNK), F32),
        pltpu.VMEM((two_h, CHUNK), F32),
        pltpu.VMEM((two_h, CHUNK), F32),
    ]


def _ssd_est(two_h, n_slabs):
    return (2 * n_slabs * SSM_GROUPS * CHUNK * COL_BLOCK * 2 + 4 * SSM_GROUPS * CHUNK * STATE_DIM * 2
            + SSM_GROUPS * STATE_DIM * COL_BLOCK * 4 + 8 * CHUNK * two_h * 4 + 8 * 1024 * 1024)


def _ssd_forward(xs, bm, cm, dt_raw, dt_bias, a_neg, bsz, seq):
    g, t, _ = xs.shape
    nc = seq // CHUNK
    two_h = dt_raw.shape[1]
    slab = pl.BlockSpec((g, CHUNK, COL_BLOCK), lambda b, c: (0, b * nc + c, 0))
    bc = pl.BlockSpec((g, CHUNK, STATE_DIM), lambda b, c: (0, b * nc + c, 0))
    row = pl.BlockSpec((1, two_h), lambda b, c: (0, 0))
    return pl.pallas_call(
        _ssd_fwd_kernel,
        grid=(bsz, nc),
        in_specs=[slab, bc, bc, pl.BlockSpec((CHUNK, two_h), lambda b, c: (b * nc + c, 0)), row, row],
        out_specs=slab,
        out_shape=jax.ShapeDtypeStruct((g, t, COL_BLOCK), BF16),
        scratch_shapes=_ssd_scratch(two_h),
        compiler_params=_params(("parallel", "arbitrary"), _ssd_est(two_h, 2)),
        name="ssd_fwd",
    )(xs, bm, cm, dt_raw, dt_bias, a_neg)


def _ssd_backward(xs, bm, cm, dt_raw, dt_bias, a_neg, y_fwd, proj, d_skip, norm_w, bsz, seq, z_block):
    g, t, _ = xs.shape
    nc = seq // CHUNK
    two_h = dt_raw.shape[1]
    rev = lambda b, c: (0, b * nc + nc - 1 - c, 0)
    slab = pl.BlockSpec((g, CHUNK, COL_BLOCK), rev)
    bc = pl.BlockSpec((g, CHUNK, STATE_DIM), rev)
    row = pl.BlockSpec((1, two_h), lambda b, c: (0, 0))
    par = pl.BlockSpec((g, 1, COL_BLOCK), lambda b, c: (0, 0, 0))
    zb = z_block // g
    assert zb * g == z_block
    kern = functools.partial(_ssd_bwd_kernel, n_heads=two_h // 2)
    return pl.pallas_call(
        kern,
        grid=(bsz, nc),
        in_specs=[slab, bc, bc, pl.BlockSpec((CHUNK, two_h), lambda b, c: (b * nc + nc - 1 - c, 0)), row, row,
                  slab, pl.BlockSpec((g, CHUNK, COL_BLOCK), lambda b, c: (zb, b * nc + nc - 1 - c, 0)), par, par],
        out_specs=slab,
        out_shape=jax.ShapeDtypeStruct((g, t, COL_BLOCK), BF16),
        scratch_shapes=_ssd_scratch(two_h),
        compiler_params=_params(("parallel", "arbitrary"), _ssd_est(two_h, 4)),
        name="ssd_bwd",
    )(xs, bm, cm, dt_raw, dt_bias, a_neg, y_fwd, proj, d_skip, norm_w)


def _t5_buckets(rel):
    half = REL_BUCKETS // 2
    ret = (rel > 0).astype(np.int32) * half
    n = np.abs(rel)
    max_exact = half // 2
    large = max_exact + (np.log(np.maximum(n, 1) / max_exact) / np.log(REL_MAX_DIST / max_exact)
                         * (half - max_exact)).astype(np.int32)
    large = np.minimum(large, half - 1)
    return ret + np.where(n < max_exact, n, large).astype(np.int32)


def _bias_kernel(rel_bias_ref, bucket_ref, o_ref):
    h = pl.program_id(0)
    bucket = bucket_ref[...]
    acc = jnp.zeros(bucket.shape, F32)
    for b in range(REL_BUCKETS):
        acc = jnp.where(bucket == b, rel_bias_ref[b, h], acc)
    o_ref[...] = jnp.where(bucket < 0, MASK_VALUE, acc)


def _attention_bias(rel_bias):
    blk = ATTN_BLOCK
    rel = np.arange(3 * blk)[None, :] - blk - np.arange(blk)[:, None]
    bucket = np.where(np.abs(rel) <= WINDOW, _t5_buckets(rel), -1).astype(np.int32)
    heads = rel_bias.shape[1]
    return pl.pallas_call(
        _bias_kernel,
        grid=(heads,),
        in_specs=[pl.BlockSpec(memory_space=pltpu.SMEM), pl.BlockSpec((blk, 3 * blk), lambda h: (0, 0))],
        out_specs=pl.BlockSpec((None, blk, 3 * blk), lambda h: (h, 0, 0)),
        out_shape=jax.ShapeDtypeStruct((heads, blk, 3 * blk), F32),
        compiler_params=_params(("arbitrary",), 4 * 1024 * 1024),
        name="attn_bias",
    )(rel_bias, jnp.asarray(bucket))


def _attn_kernel(sink_ref, q_ref, k_ref, kp_ref, kn_ref, v_ref, vp_ref, vn_ref, bias_ref, o_ref, kw_ref, vw_ref,
                 *, tq, seq, scale):
    blk = ATTN_BLOCK
    d = ATTN_HEAD_DIM
    kw_ref[0:blk, :] = kp_ref[...]
    kw_ref[blk:blk + tq, :] = k_ref[...]
    kw_ref[blk + tq:, :] = kn_ref[...]
    vw_ref[0:blk, :] = vp_ref[...]
    vw_ref[blk:blk + tq, :] = v_ref[...]
    vw_ref[blk + tq:, :] = vn_ref[...]
    step0 = pl.program_id(1) * tq
    key_off = lax.broadcasted_iota(jnp.int32, (1, 3 * blk), 1) - blk
    for g in range(KV_HEADS):
        for i in range(tq // blk):
            kwin = kw_ref[i * blk:(i + 3) * blk, g * d:(g + 1) * d]
            vwin = vw_ref[i * blk:(i + 3) * blk, g * d:(g + 1) * d]
            q_rows = q_ref[g, i * blk:(i + 1) * blk, :]
            q4 = jnp.concatenate([q_rows[:, r * d:(r + 1) * d] for r in range(Q_PER_KV)], axis=0)
            logits = lax.dot_general(q4, kwin, (((1,), (1,)), ((), ())), preferred_element_type=F32) * scale
            logits = logits + bias_ref[g]
            key_pos = step0 + i * blk + key_off
            logits = jnp.where((key_pos >= 0) & (key_pos < seq), logits, MASK_VALUE)
            probs = []
            for r in range(Q_PER_KV):
                lg = logits[r * blk:(r + 1) * blk]
                sink = sink_ref[g * Q_PER_KV + r]
                m = jnp.maximum(jnp.max(lg, axis=-1, keepdims=True), sink)
                p = jnp.exp(lg - m)
                denom = jnp.sum(p, axis=-1, keepdims=True) + jnp.exp(sink - m)
                probs.append((p / denom).astype(BF16))
            out = jnp.dot(jnp.concatenate(probs, axis=0), vwin, preferred_element_type=F32)
            for r in range(Q_PER_KV):
                o_ref[g, i * blk:(i + 1) * blk, r * d:(r + 1) * d] = out[r * blk:(r + 1) * blk].astype(o_ref.dtype)


def _attention(proj, bias, sink, bsz, seq, q_block, k_block, v_block):
    t = proj.shape[1]
    blk = ATTN_BLOCK
    tq = _tile(seq, 512)
    nq = seq // tq
    per = tq // blk
    last = t // blk - 1
    width = Q_PER_KV * ATTN_HEAD_DIM
    assert width == COL_BLOCK and KV_HEADS * ATTN_HEAD_DIM == COL_BLOCK
    qb = q_block // KV_HEADS
    assert qb * KV_HEADS == q_block
    main = lambda cb: pl.BlockSpec((None, tq, COL_BLOCK), lambda b, i: (cb, b * nq + i, 0))
    prev = lambda cb: pl.BlockSpec((None, blk, COL_BLOCK), lambda b, i: (cb, jnp.maximum((b * nq + i) * per - 1, 0), 0))
    nxt = lambda cb: pl.BlockSpec((None, blk, COL_BLOCK), lambda b, i: (cb, jnp.minimum((b * nq + i + 1) * per, last), 0))
    kern = functools.partial(_attn_kernel, tq=tq, seq=seq, scale=ATTN_HEAD_DIM ** -0.5)
    est = (2 * KV_HEADS * tq * COL_BLOCK * 2 * 2 + 4 * (tq + 2 * blk) * COL_BLOCK * 2 + 2 * (tq + 2 * blk) * COL_BLOCK * 2
           + 2 * KV_HEADS * Q_PER_KV * blk * 3 * blk * 4 + 8 * 1024 * 1024)
    return pl.pallas_call(
        kern,
        grid=(bsz, nq),
        in_specs=[
            pl.BlockSpec(memory_space=pltpu.SMEM),
            pl.BlockSpec((KV_HEADS, tq, COL_BLOCK), lambda b, i: (qb, b * nq + i, 0)),
            main(k_block), prev(k_block), nxt(k_block),
            main(v_block), prev(v_block), nxt(v_block),
            pl.BlockSpec((KV_HEADS, Q_PER_KV * blk, 3 * blk), lambda b, i: (0, 0, 0)),
        ],
        out_specs=pl.BlockSpec((KV_HEADS, tq, COL_BLOCK), lambda b, i: (0, b * nq + i, 0)),
        out_shape=jax.ShapeDtypeStruct((KV_HEADS, t, COL_BLOCK), BF16),
        scratch_shapes=[pltpu.VMEM((tq + 2 * blk, COL_BLOCK), BF16), pltpu.VMEM((tq + 2 * blk, COL_BLOCK), BF16)],
        compiler_params=_params(("parallel", "parallel"), est),
        name="attention",
    )(sink, proj, proj, proj, proj, proj, proj, proj, bias)


def _merge_kernel(ys_ref, at_ref, ga_ref, gb_ref, ws_ref, wa_ref, o_ref):
    def branch(lhs_ref, w_ref):
        acc = None
        for c in range(lhs_ref.shape[0]):
            part = jnp.dot(lhs_ref[c], w_ref[c * COL_BLOCK:(c + 1) * COL_BLOCK, :], preferred_element_type=F32)
            acc = part if acc is None else acc + part
        return acc

    ya = branch(ys_ref, ws_ref)
    yb = branch(at_ref, wa_ref)
    merged = _sigmoid(ga_ref[...].astype(F32)) * ya + _sigmoid(gb_ref[...].astype(F32)) * yb
    o_ref[...] = merged.astype(o_ref.dtype)


def _merge(y_ssm, attn, proj, w_ssm, w_attn, gate_block):
    ks, t, _ = y_ssm.shape
    ka = attn.shape[0]
    dm = w_ssm.shape[1]
    nb = dm // COL_BLOCK
    tm = _tile(t, 1024)
    est = 2 * (ks + ka) * tm * COL_BLOCK * 2 + 2 * (ks + ka) * COL_BLOCK * COL_BLOCK * 2 + 6 * tm * COL_BLOCK * 2 + 3 * tm * COL_BLOCK * 4
    return pl.pallas_call(
        _merge_kernel,
        grid=(t // tm, nb),
        in_specs=[
            pl.BlockSpec((ks, tm, COL_BLOCK), lambda i, j: (0, i, 0)),
            pl.BlockSpec((ka, tm, COL_BLOCK), lambda i, j: (0, i, 0)),
            pl.BlockSpec((None, tm, COL_BLOCK), lambda i, j: (gate_block + j, i, 0)),
            pl.BlockSpec((None, tm, COL_BLOCK), lambda i, j: (gate_block + nb + j, i, 0)),
            pl.BlockSpec((ks * COL_BLOCK, COL_BLOCK), lambda i, j: (0, j)),
            pl.BlockSpec((ka * COL_BLOCK, COL_BLOCK), lambda i, j: (0, j)),
        ],
        out_specs=pl.BlockSpec((None, tm, COL_BLOCK), lambda i, j: (j, i, 0)),
        out_shape=jax.ShapeDtypeStruct((nb, t, COL_BLOCK), BF16),
        compiler_params=_params(("parallel", "arbitrary"), est),
        name="merge",
    )(y_ssm, attn, proj, proj, w_ssm, w_attn)


def _outproj_kernel(m_ref, w_ref, x_ref, o_ref):
    acc = x_ref[...]
    for c in range(m_ref.shape[0]):
        acc = acc + jnp.dot(m_ref[c], w_ref[c * COL_BLOCK:(c + 1) * COL_BLOCK, :], preferred_element_type=F32)
    o_ref[...] = acc


def _out_proj(merged, w_out, x):
    kb, t, _ = merged.shape
    d = w_out.shape[1]
    tm = _tile(t, 1024)
    tn = COL_BLOCK
    est = 2 * kb * tm * COL_BLOCK * 2 + 2 * kb * COL_BLOCK * tn * 2 + 4 * tm * tn * 4 + tm * tn * 4
    return pl.pallas_call(
        _outproj_kernel,
        grid=(t // tm, d // tn),
        in_specs=[
            pl.BlockSpec((kb, tm, COL_BLOCK), lambda i, j: (0, i, 0)),
            pl.BlockSpec((kb * COL_BLOCK, tn), lambda i, j: (0, j)),
            pl.BlockSpec((tm, tn), lambda i, j: (i, j)),
        ],
        out_specs=pl.BlockSpec((tm, tn), lambda i, j: (i, j)),
        out_shape=jax.ShapeDtypeStruct((t, d), F32),
        compiler_params=_params(("parallel", "arbitrary"), est),
        name="out_proj",
    )(merged, w_out, x)


def _ffn_kernel(h_ref, nw_ref, wg_ref, wu_ref, wo_ref, fw_ref, o_ref, hn_ref, acc_ref, *, final_norm):
    f = pl.program_id(1)

    @pl.when(f == 0)
    def _():
        hn_ref[...] = _rmsnorm(h_ref[...], nw_ref[...]).astype(BF16)
        acc_ref[...] = jnp.zeros_like(acc_ref)

    hn = hn_ref[...]
    gate = jnp.dot(hn, wg_ref[...], preferred_element_type=F32)
    up = jnp.dot(hn, wu_ref[...], preferred_element_type=F32)
    act = (_silu(gate) * up).astype(BF16)
    acc_ref[...] += jnp.dot(act, wo_ref[...], preferred_element_type=F32)

    @pl.when(f == pl.num_programs(1) - 1)
    def _():
        out = h_ref[...] + acc_ref[...]
        o_ref[...] = _rmsnorm(out, fw_ref[...]) if final_norm else out


def _ffn(h, norm_w, w_in, w_out, final_w, final_norm):
    t, d = h.shape
    ff = w_out.shape[0]
    tm = _tile(t, 512)
    tf = _tile(ff, 512)
    nf = ff // tf
    est = 2 * tm * d * 4 * 2 + tm * d * 2 + tm * d * 4 + 2 * 3 * d * tf * 2 + 4 * tm * tf * 4
    return pl.pallas_call(
        functools.partial(_ffn_kernel, final_norm=final_norm),
        grid=(t // tm, nf),
        in_specs=[
            pl.BlockSpec((tm, d), lambda i, f: (i, 0)),
            pl.BlockSpec((1, d), lambda i, f: (0, 0)),
            pl.BlockSpec((d, tf), lambda i, f: (0, f)),
            pl.BlockSpec((d, tf), lambda i, f: (0, nf + f)),
            pl.BlockSpec((tf, d), lambda i, f: (f, 0)),
            pl.BlockSpec((1, d), lambda i, f: (0, 0)),
        ],
        out_specs=pl.BlockSpec((tm, d), lambda i, f: (i, 0)),
        out_shape=jax.ShapeDtypeStruct((t, d), F32),
        scratch_shapes=[pltpu.VMEM((tm, d), BF16), pltpu.VMEM((tm, d), F32)],
        compiler_params=_params(("parallel", "arbitrary"), est),
        name="ffn",
    )(h, norm_w.reshape(1, d), w_in, w_in, w_out, final_w.reshape(1, d))


def _layer_weights(w_in, conv_w, conv_b, dt_bias, a_log, d_skip, ssm_norm_w, d_model):
    d_inner = ssm_norm_w.shape[0]
    n_heads = d_skip.shape[0]
    gn = SSM_GROUPS * STATE_DIM
    conv_dim = d_inner + 2 * gn
    q_dim = d_model
    kv_dim = KV_HEADS * ATTN_HEAD_DIM
    cuts = np.cumsum([d_inner, conv_dim, 2 * n_heads, q_dim, kv_dim, kv_dim, 2 * d_model])
    z_w, xbc_w, dt_w, q_w, k_w, v_w, gate_w = jnp.split(w_in, cuts[:-1], axis=1)
    w_main = jnp.concatenate([z_w, q_w, xbc_w, gate_w, k_w, v_w], axis=1).astype(BF16)
    blocks = {}
    off = 0
    for name, width in (("z", d_inner), ("q", q_dim), ("xbc", conv_dim), ("gate", 2 * d_model), ("k", kv_dim), ("v", kv_dim)):
        blocks[name] = off // COL_BLOCK
        off += width
    n_conv = conv_dim // COL_BLOCK
    conv_w3 = conv_w.reshape(CONV_WIDTH, n_conv, COL_BLOCK).transpose(1, 0, 2)
    conv_b3 = conv_b.reshape(n_conv, 1, COL_BLOCK)
    dtb = dt_bias.reshape(1, 2 * n_heads)
    a_neg = (-jnp.exp(a_log.astype(F32))).reshape(1, 2 * n_heads)
    dskip = jnp.repeat(d_skip.astype(F32), HEAD_DIM_SSM).reshape(SSM_GROUPS, 1, COL_BLOCK)
    nw = ssm_norm_w.astype(F32).reshape(SSM_GROUPS, 1, COL_BLOCK)
    return w_main, dt_w.astype(BF16), blocks, conv_w3, conv_b3, dtb, a_neg, dskip, nw


def _trunk(x, mix_norm_w, w_in, conv_w, conv_b, dt_bias, a_log, d_skip, ssm_norm_w, w_ssm_branch, rel_bias,
           attn_sink, w_attn_branch, w_out, ffn_norm_w, w_ffn_in, w_ffn_out, final_norm_w):
    bsz, seq, d_model = x.shape
    depth = w_in.shape[0]
    h = x.reshape(bsz * seq, d_model)
    bias = _attention_bias(rel_bias).reshape(KV_HEADS, Q_PER_KV * ATTN_BLOCK, 3 * ATTN_BLOCK)
    for l in range(depth):
        w_main, w_dt, blocks, conv_w3, conv_b3, dtb, a_neg, dskip, nw = _layer_weights(
            w_in[l], conv_w[l], conv_b[l], dt_bias[l], a_log[l], d_skip[l], ssm_norm_w[l], d_model)
        proj, dt_raw = _in_proj(h, mix_norm_w[l], w_main, w_dt)
        n_xs = ssm_norm_w.shape[1] // COL_BLOCK
        xs, bm, cm = _conv_silu(proj, conv_w3, conv_b3, bsz, seq, blocks["xbc"], n_xs)
        y_fwd = _ssd_forward(xs, bm, cm, dt_raw, dtb, a_neg, bsz, seq)
        y_ssm = _ssd_backward(xs, bm, cm, dt_raw, dtb, a_neg, y_fwd, proj, dskip, nw, bsz, seq, blocks["z"])
        attn = _attention(proj, bias, attn_sink[l], bsz, seq, blocks["q"], blocks["k"], blocks["v"])
        merged = _merge(y_ssm, attn, proj, w_ssm_branch[l].astype(BF16), w_attn_branch[l].astype(BF16), blocks["gate"])
        h = _out_proj(merged, w_out[l].astype(BF16), h)
        h = _ffn(h, ffn_norm_w[l], w_ffn_in[l].astype(BF16), w_ffn_out[l].astype(BF16), final_norm_w,
                 final_norm=l == depth - 1)
    return h.reshape(bsz, seq, d_model)


def kernel(x_prompt, x_sample, mix_norm_w, w_in, conv_w, conv_b, dt_bias, a_log, d_skip, ssm_norm_w, w_ssm_branch,
           rel_bias, attn_sink, w_attn_branch, w_out, ffn_norm_w, w_ffn_in, w_ffn_out, final_norm_w):
    args = (mix_norm_w, w_in, conv_w, conv_b, dt_bias, a_log, d_skip, ssm_norm_w, w_ssm_branch, rel_bias, attn_sink,
            w_attn_branch, w_out, ffn_norm_w, w_ffn_in, w_ffn_out, final_norm_w)
    return (_trunk(x_prompt, *args), _trunk(x_sample, *args))
```

```python
import functools
import math

import numpy as np
import jax
import jax.numpy as jnp
from jax import lax
from jax.experimental import pallas as pl
from jax.experimental.pallas import tpu as pltpu

F32 = jnp.float32
BF16 = jnp.bfloat16

EPS = 1e-6
HEAD_DIM_SSM = 64
STATE_DIM = 128
SSM_GROUPS = 8
CHUNK = 128
GROUP_UNROLL = 4
CONV_WIDTH = 5
ATTN_HEAD_DIM = 128
KV_HEADS = 4
Q_PER_KV = 4
WINDOW = 128
ATTN_BLOCK = 128
REL_BUCKETS = 32
REL_MAX_DIST = 128
COL_BLOCK = 512
HALO_ROWS = 16
CONV_ROWS = 128
CONV_PAD = 64
LOG2_E = math.log2(math.e)
MASK_VALUE = -1e30
V7X_VMEM_BYTES = 64 * 1024 * 1024
VMEM_CAP = 60000 * 1024


def _vmem_limit(estimate_bytes):
    return int(min(VMEM_CAP, max(estimate_bytes * 5 // 4, 16 * 1024 * 1024)))


def _params(semantics, vmem_estimate):
    return pltpu.CompilerParams(dimension_semantics=semantics,
                                vmem_limit_bytes=_vmem_limit(vmem_estimate))


def _sigmoid(x):
    return 1.0 / (1.0 + jnp.exp(-x))


def _silu(x):
    return x * _sigmoid(x)


def _softplus(x):
    return jnp.maximum(x, 0.0) + jnp.log1p(jnp.exp(-jnp.abs(x)))


def _rmsnorm(x, w):
    ms = jnp.mean(x * x, axis=-1, keepdims=True)
    return x * lax.rsqrt(ms + EPS) * w


def _tile(n, want):
    t = min(n, want)
    while n % t:
        t //= 2
    return t


def _inproj_kernel(x_ref, nw_ref, w_ref, wdt_ref, o_ref, dt_ref, xn_ref):
    @pl.when(pl.program_id(1) == 0)
    def _():
        xn = _rmsnorm(x_ref[...], nw_ref[...]).astype(BF16)
        xn_ref[...] = xn
        dt_ref[...] = jnp.dot(xn, wdt_ref[...], preferred_element_type=F32)

    out = jnp.dot(xn_ref[...], w_ref[...], preferred_element_type=F32).astype(o_ref.dtype)
    for c in range(o_ref.shape[0]):
        o_ref[c] = out[:, c * COL_BLOCK:(c + 1) * COL_BLOCK]


def _in_proj(x, norm_w, w_main, w_dt):
    t, d = x.shape
    n = w_main.shape[1]
    nb = n // COL_BLOCK
    tm = _tile(t, 1024)
    per = 2 if nb % 2 == 0 else 1
    tn = per * COL_BLOCK
    ndt = w_dt.shape[1]
    est = 2 * tm * d * 4 + tm * d * 2 + 2 * d * tn * 2 + 2 * tm * tn * 2 + tm * tn * 4 + 2 * d * ndt * 2 + 2 * tm * ndt * 4
    return pl.pallas_call(
        _inproj_kernel,
        grid=(t // tm, nb // per),
        in_specs=[
            pl.BlockSpec((tm, d), lambda i, j: (i, 0)),
            pl.BlockSpec((1, d), lambda i, j: (0, 0)),
            pl.BlockSpec((d, tn), lambda i, j: (0, j)),
            pl.BlockSpec((d, ndt), lambda i, j: (0, 0)),
        ],
        out_specs=[
            pl.BlockSpec((per, tm, COL_BLOCK), lambda i, j: (j, i, 0)),
            pl.BlockSpec((tm, ndt), lambda i, j: (i, 0)),
        ],
        out_shape=[
            jax.ShapeDtypeStruct((nb, t, COL_BLOCK), BF16),
            jax.ShapeDtypeStruct((t, ndt), F32),
        ],
        scratch_shapes=[pltpu.VMEM((tm, d), BF16)],
        compiler_params=_params(("parallel", "arbitrary"), est),
        name="in_proj",
    )(x, norm_w.reshape(1, d), w_main, w_dt)


def _shift_matrix():
    half = CONV_WIDTH // 2
    rows = np.arange((CONV_WIDTH - 1) * CONV_ROWS)[:, None]
    cols = np.arange(CONV_ROWS + 2 * CONV_PAD)[None, :]
    blk = rows // CONV_ROWS
    off = blk - half + (blk >= half)
    return jnp.asarray(cols == rows - blk * CONV_ROWS + CONV_PAD + off, BF16)


def _conv_kernel(main_ref, prev_ref, next_ref, w_ref, b_ref, shift_ref, xs_ref, bm_ref, cm_ref, win_ref, *, tr, n_xs):
    i = pl.program_id(1)
    last = pl.num_programs(1) - 1
    half = CONV_WIDTH // 2
    n_blocks = main_ref.shape[0]
    shift = shift_ref[...]
    win_ref[0:CONV_PAD - HALO_ROWS, :] = jnp.zeros((CONV_PAD - HALO_ROWS, COL_BLOCK), BF16)
    win_ref[CONV_PAD + tr + HALO_ROWS:, :] = jnp.zeros((CONV_PAD - HALO_ROWS, COL_BLOCK), BF16)

    def conv_block(c):
        halo_zero = jnp.zeros((HALO_ROWS, COL_BLOCK), BF16)
        win_ref[CONV_PAD - HALO_ROWS:CONV_PAD, :] = jnp.where(i > 0, prev_ref[c], halo_zero)
        win_ref[CONV_PAD:CONV_PAD + tr, :] = main_ref[c]
        win_ref[CONV_PAD + tr:CONV_PAD + tr + HALO_ROWS, :] = jnp.where(i < last, next_ref[c], halo_zero)
        w = w_ref[c]
        bias = b_ref[c]
        outs = []
        for r in range(tr // CONV_ROWS):
            xwin = win_ref[r * CONV_ROWS:(r + 1) * CONV_ROWS + 2 * CONV_PAD, :]
            taps = jnp.dot(shift, xwin, preferred_element_type=F32)
            acc = bias + xwin[CONV_PAD:CONV_PAD + CONV_ROWS].astype(F32) * w[half:half + 1]
            for b in range(CONV_WIDTH - 1):
                j = b + (b >= half)
                acc = acc + taps[b * CONV_ROWS:(b + 1) * CONV_ROWS] * w[j:j + 1]
            outs.append(_silu(acc))
        return outs

    def xs_body(c, carry):
        for r, act in enumerate(conv_block(c)):
            xs_ref[c, r * CONV_ROWS:(r + 1) * CONV_ROWS, :] = act.astype(xs_ref.dtype)
        return carry

    lax.fori_loop(0, n_xs, xs_body, 0)
    per = COL_BLOCK // STATE_DIM
    n_b = (n_blocks - n_xs) // 2

    def state_body(first, dst):
        def body(c, carry):
            for r, act in enumerate(conv_block(c)):
                act = act.astype(dst.dtype)
                for k in range(per):
                    dst[(c - first) * per + k, r * CONV_ROWS:(r + 1) * CONV_ROWS, :] = (
                        act[:, k * STATE_DIM:(k + 1) * STATE_DIM])
            return carry
        return body

    lax.fori_loop(n_xs, n_xs + n_b, state_body(n_xs, bm_ref), 0)
    lax.fori_loop(n_xs + n_b, n_blocks, state_body(n_xs + n_b, cm_ref), 0)


def _conv_silu(proj, conv_w, conv_b, bsz, seq, first_block, n_xs):
    t = proj.shape[1]
    n_blocks = conv_w.shape[0]
    tr = _tile(seq, 512)
    nr = seq // tr
    hb = tr // HALO_ROWS
    last_halo = t // HALO_ROWS - 1
    n_bc = (n_blocks - n_xs) // 2 * (COL_BLOCK // STATE_DIM)
    cb0 = first_block // n_blocks
    assert cb0 * n_blocks == first_block
    assert tr % CONV_ROWS == 0
    est = (2 * n_blocks * (tr + 2 * HALO_ROWS) * COL_BLOCK * 2 + 2 * (n_xs * tr * COL_BLOCK + 2 * n_bc * tr * STATE_DIM) * 2
           + (tr + 2 * CONV_PAD) * COL_BLOCK * 2 + 4 * n_blocks * 8 * COL_BLOCK * 4 + 8 * 1024 * 1024)
    shift = _shift_matrix()
    kern = functools.partial(_conv_kernel, tr=tr, n_xs=n_xs)
    return pl.pallas_call(
        kern,
        grid=(bsz, nr),
        in_specs=[
            pl.BlockSpec((n_blocks, tr, COL_BLOCK), lambda b, i: (cb0, b * nr + i, 0)),
            pl.BlockSpec((n_blocks, HALO_ROWS, COL_BLOCK),
                         lambda b, i: (cb0, jnp.maximum((b * nr + i) * hb - 1, 0), 0)),
            pl.BlockSpec((n_blocks, HALO_ROWS, COL_BLOCK),
                         lambda b, i: (cb0, jnp.minimum((b * nr + i + 1) * hb, last_halo), 0)),
            pl.BlockSpec((n_blocks, CONV_WIDTH, COL_BLOCK), lambda b, i: (0, 0, 0)),
            pl.BlockSpec((n_blocks, 1, COL_BLOCK), lambda b, i: (0, 0, 0)),
            pl.BlockSpec(shift.shape, lambda b, i: (0, 0)),
        ],
        out_specs=[
            pl.BlockSpec((n_xs, tr, COL_BLOCK), lambda b, i: (0, b * nr + i, 0)),
            pl.BlockSpec((n_bc, tr, STATE_DIM), lambda b, i: (0, b * nr + i, 0)),
            pl.BlockSpec((n_bc, tr, STATE_DIM), lambda b, i: (0, b * nr + i, 0)),
        ],
        out_shape=[
            jax.ShapeDtypeStruct((n_xs, t, COL_BLOCK), BF16),
            jax.ShapeDtypeStruct((n_bc, t, STATE_DIM), BF16),
            jax.ShapeDtypeStruct((n_bc, t, STATE_DIM), BF16),
        ],
        scratch_shapes=[pltpu.VMEM((tr + 2 * CONV_PAD, COL_BLOCK), BF16)],
        compiler_params=_params(("parallel", "parallel"), est),
        name="conv_silu",
    )(proj, proj, proj, conv_w, conv_b, shift)


def _scan_rows(v, reverse):
    n = v.shape[0]
    row = lax.broadcasted_iota(jnp.int32, v.shape, 0)
    k = 1
    while k < n:
        if reverse:
            shifted = pltpu.roll(v, n - k, axis=0)
            keep = row < n - k
        else:
            shifted = pltpu.roll(v, k, axis=0)
            keep = row >= k
        v = v + jnp.where(keep, shifted, 0.0)
        k *= 2
    return v


def _ssd_chunk_prologue(dt_ref, dtb_ref, a_ref, a_sc, at_sc, wdt_sc, reverse):
    dt = _softplus(dt_ref[...] + dtb_ref[...])
    acc = _scan_rows(dt * a_ref[...], reverse)
    a_sc[...] = acc * LOG2_E
    acc_t = acc.T
    dt_t = dt.T
    at_sc[...] = acc_t * LOG2_E - jnp.log2(dt_t)
    n = acc.shape[0]
    total = acc_t[:, 0:1] if reverse else acc_t[:, n - 1:n]
    wdt_sc[...] = jnp.exp(total - acc_t) * dt_t


def _ssd_group(g, xs_ref, bm_ref, cm_ref, st_ref, a_sc, at_sc, wdt_sc, head0, reverse):
    l = CHUNK
    xs = xs_ref[g]
    bm = bm_ref[g]
    cm = cm_ref[g]
    cb = lax.dot_general(cm, bm, (((1,), (1,)), ((), ())), preferred_element_type=F32)
    bm_t = bm.astype(F32).T
    st = st_ref[g]
    y_off = jnp.dot(cm, st.astype(BF16), preferred_element_type=F32)

    heads_per_group = COL_BLOCK // HEAD_DIM_SSM
    first = head0 + g * heads_per_group
    a_cols = pltpu.roll(a_sc[...], (2 * l - first) % l, axis=1)
    row_i = lax.broadcasted_iota(jnp.int32, (l, l), 0)
    col_i = lax.broadcasted_iota(jnp.int32, (l, l), 1)
    causal = (row_i <= col_i) if reverse else (row_i >= col_i)
    lane = lax.broadcasted_iota(jnp.int32, (l, 2 * HEAD_DIM_SSM), 1)
    low_half = lane < HEAD_DIM_SSM
    keep_low = jnp.where(low_half, 1.0, 0.0).astype(BF16)
    keep_high = jnp.where(low_half, 0.0, 1.0).astype(BF16)
    edge = 0 if reverse else l - 1

    y_slabs = []
    st_slabs = []
    for pair in range(heads_per_group // 2):
        sl = slice(pair * 2 * HEAD_DIM_SSM, (pair + 1) * 2 * HEAD_DIM_SSM)
        xs_pair = xs[:, sl]
        rhs = jnp.concatenate([xs_pair * keep_low, xs_pair * keep_high], axis=0)
        m_parts, bw_parts, a_col_parts, a_tot_parts = [], [], [], []
        for sub in range(2):
            k = pair * 2 + sub
            a_col = a_cols[:, k:k + 1]
            a_row = at_sc[pl.ds(first + k, 1), :]
            wd_row = wdt_sc[pl.ds(first + k, 1), :]
            seg = jnp.where(causal, a_col - a_row, -jnp.inf)
            m_parts.append((cb * jnp.exp2(seg)).astype(BF16))
            bw_parts.append((bm_t * wd_row).astype(BF16))
            a_col_parts.append(a_col)
            a_tot_parts.append(a_cols[edge:edge + 1, k:k + 1])
        m_pair = jnp.concatenate(m_parts, axis=1)
        bw_pair = jnp.concatenate(bw_parts, axis=1)
        a_col_pair = jnp.where(low_half, a_col_parts[0], a_col_parts[1])
        a_tot_pair = jnp.where(low_half[0:1], a_tot_parts[0], a_tot_parts[1])
        y_pair = jnp.dot(m_pair, rhs, preferred_element_type=F32) + jnp.exp2(a_col_pair) * y_off[:, sl]
        st_pair = jnp.exp2(a_tot_pair) * st[:, sl] + jnp.dot(bw_pair, rhs, preferred_element_type=F32)
        y_slabs.append(y_pair)
        st_slabs.append(st_pair)
    st_ref[g] = jnp.concatenate(st_slabs, axis=1)
    return jnp.concatenate(y_slabs, axis=1), xs


def _ssd_fwd_kernel(xs_ref, bm_ref, cm_ref, dt_ref, dtb_ref, a_ref, y_ref, st_ref, a_sc, at_sc, wdt_sc):
    @pl.when(pl.program_id(1) == 0)
    def _():
        st_ref[...] = jnp.zeros_like(st_ref)

    _ssd_chunk_prologue(dt_ref, dtb_ref, a_ref, a_sc, at_sc, wdt_sc, reverse=False)

    def body(g, carry):
        y, _ = _ssd_group(g, xs_ref, bm_ref, cm_ref, st_ref, a_sc, at_sc, wdt_sc, head0=0, reverse=False)
        y_ref[g] = y.astype(y_ref.dtype)
        return carry

    lax.fori_loop(0, SSM_GROUPS, body, 0, unroll=GROUP_UNROLL)


def _ssd_bwd_kernel(xs_ref, bm_ref, cm_ref, dt_ref, dtb_ref, a_ref, yf_ref, z_ref, dskip_ref, nw_ref, y_ref,
                    st_ref, a_sc, at_sc, wdt_sc, *, n_heads):
    @pl.when(pl.program_id(1) == 0)
    def _():
        st_ref[...] = jnp.zeros_like(st_ref)

    _ssd_chunk_prologue(dt_ref, dtb_ref, a_ref, a_sc, at_sc, wdt_sc, reverse=True)

    def body(g, carry):
        y, xs = _ssd_group(g, xs_ref, bm_ref, cm_ref, st_ref, a_sc, at_sc, wdt_sc, head0=n_heads, reverse=True)
        y = y + yf_ref[g].astype(F32) + xs.astype(F32) * dskip_ref[g]
        y = y * _silu(z_ref[g].astype(F32))
        y_ref[g] = _rmsnorm(y, nw_ref[g]).astype(y_ref.dtype)
        return carry

    lax.fori_loop(0, SSM_GROUPS, body, 0, unroll=GROUP_UNROLL)


def _ssd_scratch(two_h):
    return [
        pltpu.VMEM((SSM_GROUPS, STATE_DIM, COL_BLOCK), F32),
        pltpu.VMEM((CHUNK, two_h), F32),
        pltpu.VMEM((two_h, CHUNK), F32),
        pltpu.VMEM((two_h, CHUNK), F32),
    ]


def _ssd_est(two_h, n_slabs):
    return (2 * n_slabs * SSM_GROUPS * CHUNK * COL_BLOCK * 2 + 4 * SSM_GROUPS * CHUNK * STATE_DIM * 2
            + SSM_GROUPS * STATE_DIM * COL_BLOCK * 4 + 8 * CHUNK * two_h * 4 + 8 * 1024 * 1024)


def _ssd_forward(xs, bm, cm, dt_raw, dt_bias, a_neg, bsz, seq):
    g, t, _ = xs.shape
    nc = seq // CHUNK
    two_h = dt_raw.shape[1]
    slab = pl.BlockSpec((g, CHUNK, COL_BLOCK), lambda b, c: (0, b * nc + c, 0))
    bc = pl.BlockSpec((g, CHUNK, STATE_DIM), lambda b, c: (0, b * nc + c, 0))
    row = pl.BlockSpec((1, two_h), lambda b, c: (0, 0))
    return pl.pallas_call(
        _ssd_fwd_kernel,
        grid=(bsz, nc),
        in_specs=[slab, bc, bc, pl.BlockSpec((CHUNK, two_h), lambda b, c: (b * nc + c, 0)), row, row],
        out_specs=slab,
        out_shape=jax.ShapeDtypeStruct((g, t, COL_BLOCK), BF16),
        scratch_shapes=_ssd_scratch(two_h),
        compiler_params=_params(("parallel", "arbitrary"), _ssd_est(two_h, 2)),
        name="ssd_fwd",
    )(xs, bm, cm, dt_raw, dt_bias, a_neg)


def _ssd_backward(xs, bm, cm, dt_raw, dt_bias, a_neg, y_fwd, proj, d_skip, norm_w, bsz, seq, z_block):
    g, t, _ = xs.shape
    nc = seq // CHUNK
    two_h = dt_raw.shape[1]
    rev = lambda b, c: (0, b * nc + nc - 1 - c, 0)
    slab = pl.BlockSpec((g, CHUNK, COL_BLOCK), rev)
    bc = pl.BlockSpec((g, CHUNK, STATE_DIM), rev)
    row = pl.BlockSpec((1, two_h), lambda b, c: (0, 0))
    par = pl.BlockSpec((g, 1, COL_BLOCK), lambda b, c: (0, 0, 0))
    zb = z_block // g
    assert zb * g == z_block
    kern = functools.partial(_ssd_bwd_kernel, n_heads=two_h // 2)
    return pl.pallas_call(
        kern,
        grid=(bsz, nc),
        in_specs=[slab, bc, bc, pl.BlockSpec((CHUNK, two_h), lambda b, c: (b * nc + nc - 1 - c, 0)), row, row,
                  slab, pl.BlockSpec((g, CHUNK, COL_BLOCK), lambda b, c: (zb, b * nc + nc - 1 - c, 0)), par, par],
        out_specs=slab,
        out_shape=jax.ShapeDtypeStruct((g, t, COL_BLOCK), BF16),
        scratch_shapes=_ssd_scratch(two_h),
        compiler_params=_params(("parallel", "arbitrary"), _ssd_est(two_h, 4)),
        name="ssd_bwd",
    )(xs, bm, cm, dt_raw, dt_bias, a_neg, y_fwd, proj, d_skip, norm_w)


def _t5_buckets(rel):
    half = REL_BUCKETS // 2
    ret = (rel > 0).astype(np.int32) * half
    n = np.abs(rel)
    max_exact = half // 2
    large = max_exact + (np.log(np.maximum(n, 1) / max_exact) / np.log(REL_MAX_DIST / max_exact)
                         * (half - max_exact)).astype(np.int32)
    large = np.minimum(large, half - 1)
    return ret + np.where(n < max_exact, n, large).astype(np.int32)


def _bias_kernel(rel_bias_ref, bucket_ref, o_ref):
    h = pl.program_id(0)
    bucket = bucket_ref[...]
    acc = jnp.zeros(bucket.shape, F32)
    for b in range(REL_BUCKETS):
        acc = jnp.where(bucket == b, rel_bias_ref[b, h], acc)
    o_ref[...] = jnp.where(bucket < 0, MASK_VALUE, acc * LOG2_E)


def _attention_bias(rel_bias):
    blk = ATTN_BLOCK
    rel = np.arange(3 * blk)[None, :] - blk - np.arange(blk)[:, None]
    bucket = np.where(np.abs(rel) <= WINDOW, _t5_buckets(rel), -1).astype(np.int32)
    heads = rel_bias.shape[1]
    return pl.pallas_call(
        _bias_kernel,
        grid=(heads,),
        in_specs=[pl.BlockSpec(memory_space=pltpu.SMEM), pl.BlockSpec((blk, 3 * blk), lambda h: (0, 0))],
        out_specs=pl.BlockSpec((None, blk, 3 * blk), lambda h: (h, 0, 0)),
        out_shape=jax.ShapeDtypeStruct((heads, blk, 3 * blk), F32),
        compiler_params=_params(("arbitrary",), 4 * 1024 * 1024),
        name="attn_bias",
    )(rel_bias, jnp.asarray(bucket))


def _attn_kernel(sink_ref, q_ref, k_ref, kp_ref, kn_ref, v_ref, vp_ref, vn_ref, bias_ref, o_ref, kw_ref, vw_ref,
                 *, tq, seq, scale):
    blk = ATTN_BLOCK
    d = ATTN_HEAD_DIM
    kw_ref[0:blk, :] = kp_ref[...]
    kw_ref[blk:blk + tq, :] = k_ref[...]
    kw_ref[blk + tq:, :] = kn_ref[...]
    vw_ref[0:blk, :] = vp_ref[...]
    vw_ref[blk:blk + tq, :] = v_ref[...]
    vw_ref[blk + tq:, :] = vn_ref[...]
    step0 = pl.program_id(1) * tq
    key_off = lax.broadcasted_iota(jnp.int32, (1, 3 * blk), 1) - blk
    n_sub = tq // blk
    for g in range(KV_HEADS):
        for i in range(n_sub):
            kwin = kw_ref[i * blk:(i + 3) * blk, g * d:(g + 1) * d]
            vwin = vw_ref[i * blk:(i + 3) * blk, g * d:(g + 1) * d]
            q_rows = q_ref[g, i * blk:(i + 1) * blk, :]
            q4 = jnp.concatenate([q_rows[:, r * d:(r + 1) * d] for r in range(Q_PER_KV)], axis=0)
            logits = lax.dot_general(q4, kwin, (((1,), (1,)), ((), ())), preferred_element_type=F32)
            logits = logits * (scale * LOG2_E) + bias_ref[g]
            if i == 0 or i == n_sub - 1:
                key_pos = step0 + i * blk + key_off
                logits = jnp.where((key_pos >= 0) & (key_pos < seq), logits, MASK_VALUE)
            probs, inv = [], []
            for r in range(Q_PER_KV):
                lg = logits[r * blk:(r + 1) * blk]
                sink = sink_ref[g * Q_PER_KV + r] * LOG2_E
                m = jnp.maximum(jnp.max(lg, axis=-1, keepdims=True), sink)
                p = jnp.exp2(lg - m)
                inv.append(1.0 / (jnp.sum(p, axis=-1, keepdims=True) + jnp.exp2(sink - m)))
                probs.append(p.astype(BF16))
            out = jnp.dot(jnp.concatenate(probs, axis=0), vwin, preferred_element_type=F32)
            for r in range(Q_PER_KV):
                o_ref[g, i * blk:(i + 1) * blk, r * d:(r + 1) * d] = (
                    out[r * blk:(r + 1) * blk] * inv[r]).astype(o_ref.dtype)


def _attention(proj, bias, sink, bsz, seq, q_block, k_block, v_block):
    t = proj.shape[1]
    blk = ATTN_BLOCK
    tq = _tile(seq, 512)
    nq = seq // tq
    per = tq // blk
    last = t // blk - 1
    width = Q_PER_KV * ATTN_HEAD_DIM
    assert width == COL_BLOCK and KV_HEADS * ATTN_HEAD_DIM == COL_BLOCK
    qb = q_block // KV_HEADS
    assert qb * KV_HEADS == q_block
    main = lambda cb: pl.BlockSpec((None, tq, COL_BLOCK), lambda b, i: (cb, b * nq + i, 0))
    prev = lambda cb: pl.BlockSpec((None, blk, COL_BLOCK), lambda b, i: (cb, jnp.maximum((b * nq + i) * per - 1, 0), 0))
    nxt = lambda cb: pl.BlockSpec((None, blk, COL_BLOCK), lambda b, i: (cb, jnp.minimum((b * nq + i + 1) * per, last), 0))
    kern = functools.partial(_attn_kernel, tq=tq, seq=seq, scale=ATTN_HEAD_DIM ** -0.5)
    est = (2 * KV_HEADS * tq * COL_BLOCK * 2 * 2 + 4 * (tq + 2 * blk) * COL_BLOCK * 2 + 2 * (tq + 2 * blk) * COL_BLOCK * 2
           + 2 * KV_HEADS * Q_PER_KV * blk * 3 * blk * 4 + 8 * 1024 * 1024)
    return pl.pallas_call(
        kern,
        grid=(bsz, nq),
        in_specs=[
            pl.BlockSpec(memory_space=pltpu.SMEM),
            pl.BlockSpec((KV_HEADS, tq, COL_BLOCK), lambda b, i: (qb, b * nq + i, 0)),
            main(k_block), prev(k_block), nxt(k_block),
            main(v_block), prev(v_block), nxt(v_block),
            pl.BlockSpec((KV_HEADS, Q_PER_KV * blk, 3 * blk), lambda b, i: (0, 0, 0)),
        ],
        out_specs=pl.BlockSpec((KV_HEADS, tq, COL_BLOCK), lambda b, i: (0, b * nq + i, 0)),
        out_shape=jax.ShapeDtypeStruct((KV_HEADS, t, COL_BLOCK), BF16),
        scratch_shapes=[pltpu.VMEM((tq + 2 * blk, COL_BLOCK), BF16), pltpu.VMEM((tq + 2 * blk, COL_BLOCK), BF16)],
        compiler_params=_params(("parallel", "parallel"), est),
        name="attention",
    )(sink, proj, proj, proj, proj, proj, proj, proj, bias)


def _merge_kernel(ys_ref, at_ref, ga_ref, gb_ref, ws_ref, wa_ref, o_ref):
    def branch(lhs_ref, w_ref):
        acc = None
        for c in range(lhs_ref.shape[0]):
            part = jnp.dot(lhs_ref[c], w_ref[c * COL_BLOCK:(c + 1) * COL_BLOCK, :], preferred_element_type=F32)
            acc = part if acc is None else acc + part
        return acc

    ya = branch(ys_ref, ws_ref)
    yb = branch(at_ref, wa_ref)
    merged = _sigmoid(ga_ref[...].astype(F32)) * ya + _sigmoid(gb_ref[...].astype(F32)) * yb
    o_ref[...] = merged.astype(o_ref.dtype)


def _merge(y_ssm, attn, proj, w_ssm, w_attn, gate_block):
    ks, t, _ = y_ssm.shape
    ka = attn.shape[0]
    dm = w_ssm.shape[1]
    nb = dm // COL_BLOCK
    tm = _tile(t, 1024)
    est = 2 * (ks + ka) * tm * COL_BLOCK * 2 + 2 * (ks + ka) * COL_BLOCK * COL_BLOCK * 2 + 6 * tm * COL_BLOCK * 2 + 3 * tm * COL_BLOCK * 4
    return pl.pallas_call(
        _merge_kernel,
        grid=(t // tm, nb),
        in_specs=[
            pl.BlockSpec((ks, tm, COL_BLOCK), lambda i, j: (0, i, 0)),
            pl.BlockSpec((ka, tm, COL_BLOCK), lambda i, j: (0, i, 0)),
            pl.BlockSpec((None, tm, COL_BLOCK), lambda i, j: (gate_block + j, i, 0)),
            pl.BlockSpec((None, tm, COL_BLOCK), lambda i, j: (gate_block + nb + j, i, 0)),
            pl.BlockSpec((ks * COL_BLOCK, COL_BLOCK), lambda i, j: (0, j)),
            pl.BlockSpec((ka * COL_BLOCK, COL_BLOCK), lambda i, j: (0, j)),
        ],
        out_specs=pl.BlockSpec((None, tm, COL_BLOCK), lambda i, j: (j, i, 0)),
        out_shape=jax.ShapeDtypeStruct((nb, t, COL_BLOCK), BF16),
        compiler_params=_params(("parallel", "arbitrary"), est),
        name="merge",
    )(y_ssm, attn, proj, proj, w_ssm, w_attn)


def _outproj_kernel(m_ref, w_ref, x_ref, o_ref):
    acc = x_ref[...]
    for c in range(m_ref.shape[0]):
        acc = acc + jnp.dot(m_ref[c], w_ref[c * COL_BLOCK:(c + 1) * COL_BLOCK, :], preferred_element_type=F32)
    o_ref[...] = acc


def _out_proj(merged, w_out, x):
    kb, t, _ = merged.shape
    d = w_out.shape[1]
    tm = _tile(t, 512)
    tn = d
    est = 2 * kb * tm * COL_BLOCK * 2 + 2 * kb * COL_BLOCK * tn * 2 + 4 * tm * tn * 4 + tm * tn * 4
    return pl.pallas_call(
        _outproj_kernel,
        grid=(t // tm, d // tn),
        in_specs=[
            pl.BlockSpec((kb, tm, COL_BLOCK), lambda i, j: (0, i, 0)),
            pl.BlockSpec((kb * COL_BLOCK, tn), lambda i, j: (0, j)),
            pl.BlockSpec((tm, tn), lambda i, j: (i, j)),
        ],
        out_specs=pl.BlockSpec((tm, tn), lambda i, j: (i, j)),
        out_shape=jax.ShapeDtypeStruct((t, d), F32),
        compiler_params=_params(("parallel", "arbitrary"), est),
        name="out_proj",
    )(merged, w_out, x)


def _ffn_kernel(h_ref, nw_ref, wg_ref, wu_ref, wo_ref, fw_ref, o_ref, hn_ref, acc_ref, *, final_norm):
    f = pl.program_id(1)

    @pl.when(f == 0)
    def _():
        hn_ref[...] = _rmsnorm(h_ref[...], nw_ref[...]).astype(BF16)
        acc_ref[...] = jnp.zeros_like(acc_ref)

    hn = hn_ref[...]
    gate = jnp.dot(hn, wg_ref[...], preferred_element_type=F32)
    up = jnp.dot(hn, wu_ref[...], preferred_element_type=F32)
    act = (_silu(gate) * up).astype(BF16)
    acc_ref[...] += jnp.dot(act, wo_ref[...], preferred_element_type=F32)

    @pl.when(f == pl.num_programs(1) - 1)
    def _():
        out = h_ref[...] + acc_ref[...]
        o_ref[...] = _rmsnorm(out, fw_ref[...]) if final_norm else out


def _ffn(h, norm_w, w_in, w_out, final_w, final_norm):
    t, d = h.shape
    ff = w_out.shape[0]
    tm = _tile(t, 512)
    tf = _tile(ff, 512)
    nf = ff // tf
    est = 2 * tm * d * 4 * 2 + tm * d * 2 + tm * d * 4 + 2 * 3 * d * tf * 2 + 4 * tm * tf * 4
    return pl.pallas_call(
        functools.partial(_ffn_kernel, final_norm=final_norm),
        grid=(t // tm, nf),
        in_specs=[
            pl.BlockSpec((tm, d), lambda i, f: (i, 0)),
            pl.BlockSpec((1, d), lambda i, f: (0, 0)),
            pl.BlockSpec((d, tf), lambda i, f: (0, f)),
            pl.BlockSpec((d, tf), lambda i, f: (0, nf + f)),
            pl.BlockSpec((tf, d), lambda i, f: (f, 0)),
            pl.BlockSpec((1, d), lambda i, f: (0, 0)),
        ],
        out_specs=pl.BlockSpec((tm, d), lambda i, f: (i, 0)),
        out_shape=jax.ShapeDtypeStruct((t, d), F32),
        scratch_shapes=[pltpu.VMEM((tm, d), BF16), pltpu.VMEM((tm, d), F32)],
        compiler_params=_params(("parallel", "arbitrary"), est),
        name="ffn",
    )(h, norm_w.reshape(1, d), w_in, w_in, w_out, final_w.reshape(1, d))


def _layer_weights(w_in, conv_w, conv_b, dt_bias, a_log, d_skip, ssm_norm_w, d_model):
    d_inner = ssm_norm_w.shape[0]
    n_heads = d_skip.shape[0]
    gn = SSM_GROUPS * STATE_DIM
    conv_dim = d_inner + 2 * gn
    q_dim = d_model
    kv_dim = KV_HEADS * ATTN_HEAD_DIM
    cuts = np.cumsum([d_inner, conv_dim, 2 * n_heads, q_dim, kv_dim, kv_dim, 2 * d_model])
    z_w, xbc_w, dt_w, q_w, k_w, v_w, gate_w = jnp.split(w_in, cuts[:-1], axis=1)
    w_main = jnp.concatenate([z_w, q_w, xbc_w, gate_w, k_w, v_w], axis=1).astype(BF16)
    blocks = {}
    off = 0
    for name, width in (("z", d_inner), ("q", q_dim), ("xbc", conv_dim), ("gate", 2 * d_model), ("k", kv_dim), ("v", kv_dim)):
        blocks[name] = off // COL_BLOCK
        off += width
    n_conv = conv_dim // COL_BLOCK
    conv_w3 = conv_w.reshape(CONV_WIDTH, n_conv, COL_BLOCK).transpose(1, 0, 2)
    conv_b3 = conv_b.reshape(n_conv, 1, COL_BLOCK)
    dtb = dt_bias.reshape(1, 2 * n_heads)
    a_neg = (-jnp.exp(a_log.astype(F32))).reshape(1, 2 * n_heads)
    dskip = jnp.repeat(d_skip.astype(F32), HEAD_DIM_SSM).reshape(SSM_GROUPS, 1, COL_BLOCK)
    nw = ssm_norm_w.astype(F32).reshape(SSM_GROUPS, 1, COL_BLOCK)
    return w_main, dt_w.astype(BF16), blocks, conv_w3, conv_b3, dtb, a_neg, dskip, nw


def _trunk(x, mix_norm_w, w_in, conv_w, conv_b, dt_bias, a_log, d_skip, ssm_norm_w, w_ssm_branch, rel_bias,
           attn_sink, w_attn_branch, w_out, ffn_norm_w, w_ffn_in, w_ffn_out, final_norm_w):
    bsz, seq, d_model = x.shape
    depth = w_in.shape[0]
    h = x.reshape(bsz * seq, d_model)
    bias = _attention_bias(rel_bias).reshape(KV_HEADS, Q_PER_KV * ATTN_BLOCK, 3 * ATTN_BLOCK)
    for l in range(depth):
        w_main, w_dt, blocks, conv_w3, conv_b3, dtb, a_neg, dskip, nw = _layer_weights(
            w_in[l], conv_w[l], conv_b[l], dt_bias[l], a_log[l], d_skip[l], ssm_norm_w[l], d_model)
        proj, dt_raw = _in_proj(h, mix_norm_w[l], w_main, w_dt)
        n_xs = ssm_norm_w.shape[1] // COL_BLOCK
        xs, bm, cm = _conv_silu(proj, conv_w3, conv_b3, bsz, seq, blocks["xbc"], n_xs)
        y_fwd = _ssd_forward(xs, bm, cm, dt_raw, dtb, a_neg, bsz, seq)
        y_ssm = _ssd_backward(xs, bm, cm, dt_raw, dtb, a_neg, y_fwd, proj, dskip, nw, bsz, seq, blocks["z"])
        attn = _attention(proj, bias, attn_sink[l], bsz, seq, blocks["q"], blocks["k"], blocks["v"])
        merged = _merge(y_ssm, attn, proj, w_ssm_branch[l].astype(BF16), w_attn_branch[l].astype(BF16), blocks["gate"])
        h = _out_proj(merged, w_out[l].astype(BF16), h)
        h = _ffn(h, ffn_norm_w[l], w_ffn_in[l].astype(BF16), w_ffn_out[l].astype(BF16), final_norm_w,
                 final_norm=l == depth - 1)
    return h.reshape(bsz, seq, d_model)


def kernel(x_prompt, x_sample, mix_norm_w, w_in, conv_w, conv_b, dt_bias, a_log, d_skip, ssm_norm_w, w_ssm_branch,
           rel_bias, attn_sink, w_attn_branch, w_out, ffn_norm_w, w_ffn_in, w_ffn_out, final_norm_w):
    args = (mix_norm_w, w_in, conv_w, conv_b, dt_bias, a_log, d_skip, ssm_norm_w, w_ssm_branch, rel_bias, attn_sink,
            w_attn_branch, w_out, ffn_norm_w, w_ffn_in, w_ffn_out, final_norm_w)
    return (_trunk(x_prompt, *args), _trunk(x_sample, *args))
```

```python
import functools
import math

import numpy as np
import jax
import jax.numpy as jnp
from jax import lax
from jax.experimental import pallas as pl
from jax.experimental.pallas import tpu as pltpu

F32 = jnp.float32
BF16 = jnp.bfloat16

EPS = 1e-6
HEAD_DIM_SSM = 64
STATE_DIM = 128
SSM_GROUPS = 8
CHUNK = 128
FFN_SPLIT = 2
SSD_STEP_CHUNKS = 4
CONV_WIDTH = 5
ATTN_HEAD_DIM = 128
KV_HEADS = 4
Q_PER_KV = 4
WINDOW = 128
ATTN_BLOCK = 128
REL_BUCKETS = 32
REL_MAX_DIST = 128
COL_BLOCK = 512
SUBLANES = 8
HALO_ROWS = 16
CONV_ROWS = 128
CONV_PAD = 64
LOG2_E = math.log2(math.e)
MASK_VALUE = -1e30
V7X_VMEM_BYTES = 64 * 1024 * 1024
VMEM_CAP = 60000 * 1024


def _vmem_limit(estimate_bytes):
    return int(min(VMEM_CAP, max(estimate_bytes * 5 // 4, 16 * 1024 * 1024)))


def _params(semantics, vmem_estimate):
    return pltpu.CompilerParams(dimension_semantics=semantics,
                                vmem_limit_bytes=_vmem_limit(vmem_estimate))


def _sigmoid(x):
    return 1.0 / (1.0 + jnp.exp2(x * -LOG2_E))


def _silu(x):
    return x * _sigmoid(x)


def _softplus(x):
    return jnp.maximum(x, 0.0) + jnp.log1p(jnp.exp(-jnp.abs(x)))


def _rmsnorm(x, w):
    ms = jnp.mean(x * x, axis=-1, keepdims=True)
    return x * lax.rsqrt(ms + EPS) * w


def _tile(n, want):
    t = min(n, want)
    while n % t:
        t //= 2
    return t


def _inproj_kernel(x_ref, nw_ref, w_ref, wdt_ref, o_ref, dt_ref, xn_ref):
    @pl.when(pl.program_id(1) == 0)
    def _():
        xn = _rmsnorm(x_ref[...], nw_ref[...]).astype(BF16)
        xn_ref[...] = xn
        dt_ref[...] = jnp.dot(xn, wdt_ref[...], preferred_element_type=F32)

    out = jnp.dot(xn_ref[...], w_ref[...], preferred_element_type=F32).astype(o_ref.dtype)
    for c in range(o_ref.shape[0]):
        o_ref[c] = out[:, c * COL_BLOCK:(c + 1) * COL_BLOCK]


def _in_proj(x, norm_w, w_main, w_dt):
    t, d = x.shape
    n = w_main.shape[1]
    nb = n // COL_BLOCK
    tm = _tile(t, 1024)
    per = 2 if nb % 2 == 0 else 1
    tn = per * COL_BLOCK
    ndt = w_dt.shape[1]
    est = 2 * tm * d * 4 + tm * d * 2 + 2 * d * tn * 2 + 2 * tm * tn * 2 + tm * tn * 4 + 2 * d * ndt * 2 + 2 * tm * ndt * 4
    return pl.pallas_call(
        _inproj_kernel,
        grid=(t // tm, nb // per),
        in_specs=[
            pl.BlockSpec((tm, d), lambda i, j: (i, 0)),
            pl.BlockSpec((1, d), lambda i, j: (0, 0)),
            pl.BlockSpec((d, tn), lambda i, j: (0, j)),
            pl.BlockSpec((d, ndt), lambda i, j: (0, 0)),
        ],
        out_specs=[
            pl.BlockSpec((per, tm, COL_BLOCK), lambda i, j: (j, i, 0)),
            pl.BlockSpec((tm, ndt), lambda i, j: (i, 0)),
        ],
        out_shape=[
            jax.ShapeDtypeStruct((nb, t, COL_BLOCK), BF16),
            jax.ShapeDtypeStruct((t, ndt), F32),
        ],
        scratch_shapes=[pltpu.VMEM((tm, d), BF16)],
        compiler_params=_params(("parallel", "arbitrary"), est),
        name="in_proj",
    )(x, norm_w.reshape(1, d), w_main, w_dt)


def _shift_matrix():
    half = CONV_WIDTH // 2
    n_shift = CONV_WIDTH - 1
    rows = np.arange(n_shift * CONV_ROWS)[:, None]
    cols = np.arange(CONV_ROWS + 2 * CONV_PAD)[None, :]
    q, rem = rows // (n_shift * SUBLANES), rows % (n_shift * SUBLANES)
    blk, t = rem // SUBLANES, q * SUBLANES + rem % SUBLANES
    off = blk - half + (blk >= half)
    return jnp.asarray(cols == t + CONV_PAD + off, BF16)


def _conv_kernel(main_ref, prev_ref, next_ref, w_ref, b_ref, shift_ref, xs_ref, bm_ref, cm_ref, win_ref, *, tr, n_xs):
    i = pl.program_id(1)
    last = pl.num_programs(1) - 1
    half = CONV_WIDTH // 2
    n_blocks = main_ref.shape[0]
    shift = shift_ref[...]
    win_ref[0:CONV_PAD - HALO_ROWS, :] = jnp.zeros((CONV_PAD - HALO_ROWS, COL_BLOCK), BF16)
    win_ref[CONV_PAD + tr + HALO_ROWS:, :] = jnp.zeros((CONV_PAD - HALO_ROWS, COL_BLOCK), BF16)

    def conv_block(c):
        halo_zero = jnp.zeros((HALO_ROWS, COL_BLOCK), BF16)
        win_ref[CONV_PAD - HALO_ROWS:CONV_PAD, :] = jnp.where(i > 0, prev_ref[c], halo_zero)
        win_ref[CONV_PAD:CONV_PAD + tr, :] = main_ref[c]
        win_ref[CONV_PAD + tr:CONV_PAD + tr + HALO_ROWS, :] = jnp.where(i < last, next_ref[c], halo_zero)
        w = w_ref[c]
        bias = b_ref[c]
        outs = []
        for r in range(tr // CONV_ROWS):
            xwin = win_ref[r * CONV_ROWS:(r + 1) * CONV_ROWS + 2 * CONV_PAD, :]
            taps = jnp.dot(shift, xwin, preferred_element_type=F32)
            centre = xwin[CONV_PAD:CONV_PAD + CONV_ROWS].astype(F32)
            groups = []
            for q in range(CONV_ROWS // SUBLANES):
                acc = bias + centre[q * SUBLANES:(q + 1) * SUBLANES] * w[half:half + 1]
                for b in range(CONV_WIDTH - 1):
                    j = b + (b >= half)
                    row0 = (q * (CONV_WIDTH - 1) + b) * SUBLANES
                    acc = acc + taps[row0:row0 + SUBLANES] * w[j:j + 1]
                groups.append(acc)
            outs.append(_silu(jnp.concatenate(groups, axis=0)))
        return outs

    def xs_body(c, carry):
        for r, act in enumerate(conv_block(c)):
            xs_ref[c, r * CONV_ROWS:(r + 1) * CONV_ROWS, :] = act.astype(xs_ref.dtype)
        return carry

    lax.fori_loop(0, n_xs, xs_body, 0)
    per = COL_BLOCK // STATE_DIM
    n_b = (n_blocks - n_xs) // 2

    def state_body(first, dst):
        def body(c, carry):
            for r, act in enumerate(conv_block(c)):
                act = act.astype(dst.dtype)
                for k in range(per):
                    dst[(c - first) * per + k, r * CONV_ROWS:(r + 1) * CONV_ROWS, :] = (
                        act[:, k * STATE_DIM:(k + 1) * STATE_DIM])
            return carry
        return body

    lax.fori_loop(n_xs, n_xs + n_b, state_body(n_xs, bm_ref), 0)
    lax.fori_loop(n_xs + n_b, n_blocks, state_body(n_xs + n_b, cm_ref), 0)


def _conv_silu(proj, conv_w, conv_b, bsz, seq, first_block, n_xs):
    t = proj.shape[1]
    n_blocks = conv_w.shape[0]
    tr = _tile(seq, 512)
    nr = seq // tr
    hb = tr // HALO_ROWS
    last_halo = t // HALO_ROWS - 1
    n_bc = (n_blocks - n_xs) // 2 * (COL_BLOCK // STATE_DIM)
    cb0 = first_block // n_blocks
    assert cb0 * n_blocks == first_block
    assert tr % CONV_ROWS == 0
    est = (2 * n_blocks * (tr + 2 * HALO_ROWS) * COL_BLOCK * 2 + 2 * (n_xs * tr * COL_BLOCK + 2 * n_bc * tr * STATE_DIM) * 2
           + (tr + 2 * CONV_PAD) * COL_BLOCK * 2 + 4 * n_blocks * 8 * COL_BLOCK * 4 + 8 * 1024 * 1024)
    shift = _shift_matrix()
    kern = functools.partial(_conv_kernel, tr=tr, n_xs=n_xs)
    return pl.pallas_call(
        kern,
        grid=(bsz, nr),
        in_specs=[
            pl.BlockSpec((n_blocks, tr, COL_BLOCK), lambda b, i: (cb0, b * nr + i, 0)),
            pl.BlockSpec((n_blocks, HALO_ROWS, COL_BLOCK),
                         lambda b, i: (cb0, jnp.maximum((b * nr + i) * hb - 1, 0), 0)),
            pl.BlockSpec((n_blocks, HALO_ROWS, COL_BLOCK),
                         lambda b, i: (cb0, jnp.minimum((b * nr + i + 1) * hb, last_halo), 0)),
            pl.BlockSpec((n_blocks, CONV_WIDTH, COL_BLOCK), lambda b, i: (0, 0, 0)),
            pl.BlockSpec((n_blocks, 1, COL_BLOCK), lambda b, i: (0, 0, 0)),
            pl.BlockSpec(shift.shape, lambda b, i: (0, 0)),
        ],
        out_specs=[
            pl.BlockSpec((n_xs, tr, COL_BLOCK), lambda b, i: (0, b * nr + i, 0)),
            pl.BlockSpec((n_bc, tr, STATE_DIM), lambda b, i: (0, b * nr + i, 0)),
            pl.BlockSpec((n_bc, tr, STATE_DIM), lambda b, i: (0, b * nr + i, 0)),
        ],
        out_shape=[
            jax.ShapeDtypeStruct((n_xs, t, COL_BLOCK), BF16),
            jax.ShapeDtypeStruct((n_bc, t, STATE_DIM), BF16),
            jax.ShapeDtypeStruct((n_bc, t, STATE_DIM), BF16),
        ],
        scratch_shapes=[pltpu.VMEM((tr + 2 * CONV_PAD, COL_BLOCK), BF16)],
        compiler_params=_params(("parallel", "parallel"), est),
        name="conv_silu",
    )(proj, proj, proj, conv_w, conv_b, shift)


def _scan_rows(v, reverse):
    n = v.shape[0]
    row = lax.broadcasted_iota(jnp.int32, v.shape, 0)
    k = 1
    while k < n:
        if reverse:
            shifted = pltpu.roll(v, n - k, axis=0)
            keep = row < n - k
        else:
            shifted = pltpu.roll(v, k, axis=0)
            keep = row >= k
        v = v + jnp.where(keep, shifted, 0.0)
        k *= 2
    return v


def _ssd_chunk_prologue(dt_raw, dtb_ref, a_ref, a_sc, at_sc, wdt_sc, reverse):
    dt = _softplus(dt_raw + dtb_ref[...])
    acc = _scan_rows(dt * a_ref[...], reverse)
    a_sc[...] = acc * LOG2_E
    acc_t = acc.T
    dt_t = dt.T
    at_sc[...] = acc_t * LOG2_E - jnp.log2(dt_t)
    n = acc.shape[0]
    total = acc_t[:, 0:1] if reverse else acc_t[:, n - 1:n]
    wdt_sc[...] = jnp.exp(total - acc_t) * dt_t


def _ssd_group(g, rows, xs_ref, bm_ref, cm_ref, st_ref, a_sc, at_sc, wdt_sc, head0, reverse):
    l = CHUNK
    xs = xs_ref[g, rows, :]
    bm = bm_ref[g, rows, :]
    cm = cm_ref[g, rows, :]
    cb = lax.dot_general(cm, bm, (((1,), (1,)), ((), ())), preferred_element_type=F32)
    bm_t = bm.astype(F32).T
    st = st_ref[g]
    y_off = jnp.dot(cm, st.astype(BF16), preferred_element_type=F32)

    heads_per_group = COL_BLOCK // HEAD_DIM_SSM
    first = head0 + g * heads_per_group
    a_all = a_sc[...]
    row_i = lax.broadcasted_iota(jnp.int32, (l, l), 0)
    col_i = lax.broadcasted_iota(jnp.int32, (l, l), 1)
    causal = (row_i <= col_i) if reverse else (row_i >= col_i)
    lane = lax.broadcasted_iota(jnp.int32, (l, 2 * HEAD_DIM_SSM), 1)
    low_half = lane < HEAD_DIM_SSM
    keep_low = jnp.where(low_half, 1.0, 0.0).astype(BF16)
    keep_high = jnp.where(low_half, 0.0, 1.0).astype(BF16)
    edge = 0 if reverse else l - 1

    y_slabs = []
    st_slabs = []
    for pair in range(heads_per_group // 2):
        sl = slice(pair * 2 * HEAD_DIM_SSM, (pair + 1) * 2 * HEAD_DIM_SSM)
        xs_pair = xs[:, sl]
        rhs = jnp.concatenate([xs_pair * keep_low, xs_pair * keep_high], axis=0)
        m_parts, bw_parts, a_col_parts, a_tot_parts = [], [], [], []
        for sub in range(2):
            h = first + pair * 2 + sub
            a_col = a_all[:, h:h + 1]
            a_row = at_sc[h:h + 1, :]
            wd_row = wdt_sc[h:h + 1, :]
            seg = jnp.where(causal, a_col - a_row, -jnp.inf)
            m_parts.append((cb * jnp.exp2(seg)).astype(BF16))
            bw_parts.append((bm_t * wd_row).astype(BF16))
            a_col_parts.append(a_col)
            a_tot_parts.append(a_all[edge:edge + 1, h:h + 1])
        m_pair = jnp.concatenate(m_parts, axis=1)
        bw_pair = jnp.concatenate(bw_parts, axis=1)
        a_col_pair = jnp.where(low_half, a_col_parts[0], a_col_parts[1])
        a_tot_pair = jnp.where(low_half[0:1], a_tot_parts[0], a_tot_parts[1])
        y_pair = jnp.dot(m_pair, rhs, preferred_element_type=F32) + jnp.exp2(a_col_pair) * y_off[:, sl]
        st_pair = jnp.exp2(a_tot_pair) * st[:, sl] + jnp.dot(bw_pair, rhs, preferred_element_type=F32)
        y_slabs.append(y_pair)
        st_slabs.append(st_pair)
    st_ref[g] = jnp.concatenate(st_slabs, axis=1)
    return jnp.concatenate(y_slabs, axis=1), xs


def _chunk_rows(index):
    return pl.ds(pl.multiple_of(index * CHUNK, CHUNK), CHUNK)


def _ssd_fwd_kernel(xs_ref, bm_ref, cm_ref, dt_ref, dtb_ref, a_ref, y_ref, st_ref, a_sc, at_sc, wdt_sc):
    @pl.when(pl.program_id(1) == 0)
    def _():
        st_ref[...] = jnp.zeros_like(st_ref)

    def chunk(c, carry):
        rows = _chunk_rows(c)
        _ssd_chunk_prologue(dt_ref[rows, :], dtb_ref, a_ref, a_sc, at_sc, wdt_sc, reverse=False)
        for g in range(SSM_GROUPS):
            y, _ = _ssd_group(g, rows, xs_ref, bm_ref, cm_ref, st_ref, a_sc, at_sc, wdt_sc, head0=0, reverse=False)
            y_ref[g, rows, :] = y.astype(y_ref.dtype)
        return carry

    lax.fori_loop(0, dt_ref.shape[0] // CHUNK, chunk, 0)


def _ssd_bwd_kernel(xs_ref, bm_ref, cm_ref, dt_ref, dtb_ref, a_ref, yf_ref, z_ref, dskip_ref, nw_ref, y_ref,
                    st_ref, a_sc, at_sc, wdt_sc, *, n_heads):
    @pl.when(pl.program_id(1) == 0)
    def _():
        st_ref[...] = jnp.zeros_like(st_ref)

    n_chunks = dt_ref.shape[0] // CHUNK

    def chunk(it, carry):
        rows = _chunk_rows(n_chunks - 1 - it)
        _ssd_chunk_prologue(dt_ref[rows, :], dtb_ref, a_ref, a_sc, at_sc, wdt_sc, reverse=True)
        for g in range(SSM_GROUPS):
            y, xs = _ssd_group(g, rows, xs_ref, bm_ref, cm_ref, st_ref, a_sc, at_sc, wdt_sc, head0=n_heads,
                               reverse=True)
            y = y + yf_ref[g, rows, :].astype(F32) + xs.astype(F32) * dskip_ref[g]
            y = y * _silu(z_ref[g, rows, :].astype(F32))
            y_ref[g, rows, :] = _rmsnorm(y, nw_ref[g]).astype(y_ref.dtype)
        return carry

    lax.fori_loop(0, n_chunks, chunk, 0)


def _ssd_scratch(two_h):
    return [
        pltpu.VMEM((SSM_GROUPS, STATE_DIM, COL_BLOCK), F32),
        pltpu.VMEM((CHUNK, two_h), F32),
        pltpu.VMEM((two_h, CHUNK), F32),
        pltpu.VMEM((two_h, CHUNK), F32),
    ]


def _ssd_est(two_h, n_slabs, rows):
    return (2 * n_slabs * SSM_GROUPS * rows * COL_BLOCK * 2 + 4 * SSM_GROUPS * rows * STATE_DIM * 2
            + SSM_GROUPS * STATE_DIM * COL_BLOCK * 4 + 2 * rows * two_h * 4 + 6 * CHUNK * two_h * 4 + 8 * 1024 * 1024)


def _ssd_forward(xs, bm, cm, dt_raw, dt_bias, a_neg, bsz, seq):
    g, t, _ = xs.shape
    rows = _tile(seq, SSD_STEP_CHUNKS * CHUNK)
    nc = seq // rows
    two_h = dt_raw.shape[1]
    slab = pl.BlockSpec((g, rows, COL_BLOCK), lambda b, c: (0, b * nc + c, 0))
    bc = pl.BlockSpec((g, rows, STATE_DIM), lambda b, c: (0, b * nc + c, 0))
    row = pl.BlockSpec((1, two_h), lambda b, c: (0, 0))
    return pl.pallas_call(
        _ssd_fwd_kernel,
        grid=(bsz, nc),
        in_specs=[slab, bc, bc, pl.BlockSpec((rows, two_h), lambda b, c: (b * nc + c, 0)), row, row],
        out_specs=slab,
        out_shape=jax.ShapeDtypeStruct((g, t, COL_BLOCK), BF16),
        scratch_shapes=_ssd_scratch(two_h),
        compiler_params=_params(("parallel", "arbitrary"), _ssd_est(two_h, 2, rows)),
        name="ssd_fwd",
    )(xs, bm, cm, dt_raw, dt_bias, a_neg)


def _ssd_backward(xs, bm, cm, dt_raw, dt_bias, a_neg, y_fwd, proj, d_skip, norm_w, bsz, seq, z_block):
    g, t, _ = xs.shape
    rows = _tile(seq, SSD_STEP_CHUNKS * CHUNK)
    nc = seq // rows
    two_h = dt_raw.shape[1]
    rev = lambda b, c: (0, b * nc + nc - 1 - c, 0)
    slab = pl.BlockSpec((g, rows, COL_BLOCK), rev)
    bc = pl.BlockSpec((g, rows, STATE_DIM), rev)
    row = pl.BlockSpec((1, two_h), lambda b, c: (0, 0))
    par = pl.BlockSpec((g, 1, COL_BLOCK), lambda b, c: (0, 0, 0))
    zb = z_block // g
    assert zb * g == z_block
    kern = functools.partial(_ssd_bwd_kernel, n_heads=two_h // 2)
    return pl.pallas_call(
        kern,
        grid=(bsz, nc),
        in_specs=[slab, bc, bc, pl.BlockSpec((rows, two_h), lambda b, c: (b * nc + nc - 1 - c, 0)), row, row,
                  slab, pl.BlockSpec((g, rows, COL_BLOCK), lambda b, c: (zb, b * nc + nc - 1 - c, 0)), par, par],
        out_specs=slab,
        out_shape=jax.ShapeDtypeStruct((g, t, COL_BLOCK), BF16),
        scratch_shapes=_ssd_scratch(two_h),
        compiler_params=_params(("parallel", "arbitrary"), _ssd_est(two_h, 4, rows)),
        name="ssd_bwd",
    )(xs, bm, cm, dt_raw, dt_bias, a_neg, y_fwd, proj, d_skip, norm_w)


def _t5_buckets(rel):
    half = REL_BUCKETS // 2
    ret = (rel > 0).astype(np.int32) * half
    n = np.abs(rel)
    max_exact = half // 2
    large = max_exact + (np.log(np.maximum(n, 1) / max_exact) / np.log(REL_MAX_DIST / max_exact)
                         * (half - max_exact)).astype(np.int32)
    large = np.minimum(large, half - 1)
    return ret + np.where(n < max_exact, n, large).astype(np.int32)


def _bias_kernel(rel_bias_ref, bucket_ref, o_ref):
    h = pl.program_id(0)
    bucket = bucket_ref[...]
    acc = jnp.zeros(bucket.shape, F32)
    for b in range(REL_BUCKETS):
        acc = jnp.where(bucket == b, rel_bias_ref[b, h], acc)
    o_ref[...] = jnp.where(bucket < 0, MASK_VALUE, acc * LOG2_E)


def _attention_bias(rel_bias):
    blk = ATTN_BLOCK
    rel = np.arange(3 * blk)[None, :] - blk - np.arange(blk)[:, None]
    bucket = np.where(np.abs(rel) <= WINDOW, _t5_buckets(rel), -1).astype(np.int32)
    heads = rel_bias.shape[1]
    return pl.pallas_call(
        _bias_kernel,
        grid=(heads,),
        in_specs=[pl.BlockSpec(memory_space=pltpu.SMEM), pl.BlockSpec((blk, 3 * blk), lambda h: (0, 0))],
        out_specs=pl.BlockSpec((None, blk, 3 * blk), lambda h: (h, 0, 0)),
        out_shape=jax.ShapeDtypeStruct((heads, blk, 3 * blk), F32),
        compiler_params=_params(("arbitrary",), 4 * 1024 * 1024),
        name="attn_bias",
    )(rel_bias, jnp.asarray(bucket))


def _attn_kernel(sink_ref, q_ref, k_ref, kp_ref, kn_ref, v_ref, vp_ref, vn_ref, bias_ref, o_ref, kw_ref, vw_ref,
                 *, tq, seq, scale):
    blk = ATTN_BLOCK
    d = ATTN_HEAD_DIM
    kw_ref[0:blk, :] = kp_ref[...]
    kw_ref[blk:blk + tq, :] = k_ref[...]
    kw_ref[blk + tq:, :] = kn_ref[...]
    vw_ref[0:blk, :] = vp_ref[...]
    vw_ref[blk:blk + tq, :] = v_ref[...]
    vw_ref[blk + tq:, :] = vn_ref[...]
    step0 = pl.program_id(1) * tq
    key_off = lax.broadcasted_iota(jnp.int32, (1, 3 * blk), 1) - blk
    n_sub = tq // blk
    for g in range(KV_HEADS):
        for i in range(n_sub):
            kwin = kw_ref[i * blk:(i + 3) * blk, g * d:(g + 1) * d]
            vwin = vw_ref[i * blk:(i + 3) * blk, g * d:(g + 1) * d]
            q_rows = q_ref[g, i * blk:(i + 1) * blk, :]
            q4 = jnp.concatenate([q_rows[:, r * d:(r + 1) * d] for r in range(Q_PER_KV)], axis=0)
            logits = lax.dot_general(q4, kwin, (((1,), (1,)), ((), ())), preferred_element_type=F32)
            logits = logits * (scale * LOG2_E) + bias_ref[g]
            if i == 0 or i == n_sub - 1:
                key_pos = step0 + i * blk + key_off
                logits = jnp.where((key_pos >= 0) & (key_pos < seq), logits, MASK_VALUE)
            probs, inv = [], []
            for r in range(Q_PER_KV):
                lg = logits[r * blk:(r + 1) * blk]
                sink = sink_ref[g * Q_PER_KV + r] * LOG2_E
                m = jnp.maximum(jnp.max(lg, axis=-1, keepdims=True), sink)
                p = jnp.exp2(lg - m)
                inv.append(1.0 / (jnp.sum(p, axis=-1, keepdims=True) + jnp.exp2(sink - m)))
                probs.append(p.astype(BF16))
            out = jnp.dot(jnp.concatenate(probs, axis=0), vwin, preferred_element_type=F32)
            for r in range(Q_PER_KV):
                o_ref[g, i * blk:(i + 1) * blk, r * d:(r + 1) * d] = (
                    out[r * blk:(r + 1) * blk] * inv[r]).astype(o_ref.dtype)


def _attention(proj, bias, sink, bsz, seq, q_block, k_block, v_block):
    t = proj.shape[1]
    blk = ATTN_BLOCK
    tq = _tile(seq, 512)
    nq = seq // tq
    per = tq // blk
    last = t // blk - 1
    width = Q_PER_KV * ATTN_HEAD_DIM
    assert width == COL_BLOCK and KV_HEADS * ATTN_HEAD_DIM == COL_BLOCK
    qb = q_block // KV_HEADS
    assert qb * KV_HEADS == q_block
    main = lambda cb: pl.BlockSpec((None, tq, COL_BLOCK), lambda b, i: (cb, b * nq + i, 0))
    prev = lambda cb: pl.BlockSpec((None, blk, COL_BLOCK), lambda b, i: (cb, jnp.maximum((b * nq + i) * per - 1, 0), 0))
    nxt = lambda cb: pl.BlockSpec((None, blk, COL_BLOCK), lambda b, i: (cb, jnp.minimum((b * nq + i + 1) * per, last), 0))
    kern = functools.partial(_attn_kernel, tq=tq, seq=seq, scale=ATTN_HEAD_DIM ** -0.5)
    est = (2 * KV_HEADS * tq * COL_BLOCK * 2 * 2 + 4 * (tq + 2 * blk) * COL_BLOCK * 2 + 2 * (tq + 2 * blk) * COL_BLOCK * 2
           + 2 * KV_HEADS * Q_PER_KV * blk * 3 * blk * 4 + 8 * 1024 * 1024)
    return pl.pallas_call(
        kern,
        grid=(bsz, nq),
        in_specs=[
            pl.BlockSpec(memory_space=pltpu.SMEM),
            pl.BlockSpec((KV_HEADS, tq, COL_BLOCK), lambda b, i: (qb, b * nq + i, 0)),
            main(k_block), prev(k_block), nxt(k_block),
            main(v_block), prev(v_block), nxt(v_block),
            pl.BlockSpec((KV_HEADS, Q_PER_KV * blk, 3 * blk), lambda b, i: (0, 0, 0)),
        ],
        out_specs=pl.BlockSpec((KV_HEADS, tq, COL_BLOCK), lambda b, i: (0, b * nq + i, 0)),
        out_shape=jax.ShapeDtypeStruct((KV_HEADS, t, COL_BLOCK), BF16),
        scratch_shapes=[pltpu.VMEM((tq + 2 * blk, COL_BLOCK), BF16), pltpu.VMEM((tq + 2 * blk, COL_BLOCK), BF16)],
        compiler_params=_params(("parallel", "parallel"), est),
        name="attention",
    )(sink, proj, proj, proj, proj, proj, proj, proj, bias)


def _merge_kernel(ys_ref, at_ref, ga_ref, gb_ref, ws_ref, wa_ref, o_ref):
    def branch(lhs_ref, w_ref):
        acc = None
        for c in range(lhs_ref.shape[0]):
            part = jnp.dot(lhs_ref[c], w_ref[c * COL_BLOCK:(c + 1) * COL_BLOCK, :], preferred_element_type=F32)
            acc = part if acc is None else acc + part
        return acc

    ya = branch(ys_ref, ws_ref)
    yb = branch(at_ref, wa_ref)
    merged = _sigmoid(ga_ref[...].astype(F32)) * ya + _sigmoid(gb_ref[...].astype(F32)) * yb
    o_ref[...] = merged.astype(o_ref.dtype)


def _merge(y_ssm, attn, proj, w_ssm, w_attn, gate_block):
    ks, t, _ = y_ssm.shape
    ka = attn.shape[0]
    dm = w_ssm.shape[1]
    nb = dm // COL_BLOCK
    tm = _tile(t, 1024)
    est = 2 * (ks + ka) * tm * COL_BLOCK * 2 + 2 * (ks + ka) * COL_BLOCK * COL_BLOCK * 2 + 6 * tm * COL_BLOCK * 2 + 3 * tm * COL_BLOCK * 4
    return pl.pallas_call(
        _merge_kernel,
        grid=(t // tm, nb),
        in_specs=[
            pl.BlockSpec((ks, tm, COL_BLOCK), lambda i, j: (0, i, 0)),
            pl.BlockSpec((ka, tm, COL_BLOCK), lambda i, j: (0, i, 0)),
            pl.BlockSpec((None, tm, COL_BLOCK), lambda i, j: (gate_block + j, i, 0)),
            pl.BlockSpec((None, tm, COL_BLOCK), lambda i, j: (gate_block + nb + j, i, 0)),
            pl.BlockSpec((ks * COL_BLOCK, COL_BLOCK), lambda i, j: (0, j)),
            pl.BlockSpec((ka * COL_BLOCK, COL_BLOCK), lambda i, j: (0, j)),
        ],
        out_specs=pl.BlockSpec((None, tm, COL_BLOCK), lambda i, j: (j, i, 0)),
        out_shape=jax.ShapeDtypeStruct((nb, t, COL_BLOCK), BF16),
        compiler_params=_params(("parallel", "arbitrary"), est),
        name="merge",
    )(y_ssm, attn, proj, proj, w_ssm, w_attn)


def _outproj_kernel(m_ref, w_ref, x_ref, o_ref):
    acc = x_ref[...]
    for c in range(m_ref.shape[0]):
        acc = acc + jnp.dot(m_ref[c], w_ref[c * COL_BLOCK:(c + 1) * COL_BLOCK, :], preferred_element_type=F32)
    o_ref[...] = acc


def _out_proj(merged, w_out, x):
    kb, t, _ = merged.shape
    d = w_out.shape[1]
    tm = _tile(t, 512)
    tn = d
    est = 2 * kb * tm * COL_BLOCK * 2 + 2 * kb * COL_BLOCK * tn * 2 + 4 * tm * tn * 4 + tm * tn * 4
    return pl.pallas_call(
        _outproj_kernel,
        grid=(t // tm, d // tn),
        in_specs=[
            pl.BlockSpec((kb, tm, COL_BLOCK), lambda i, j: (0, i, 0)),
            pl.BlockSpec((kb * COL_BLOCK, tn), lambda i, j: (0, j)),
            pl.BlockSpec((tm, tn), lambda i, j: (i, j)),
        ],
        out_specs=pl.BlockSpec((tm, tn), lambda i, j: (i, j)),
        out_shape=jax.ShapeDtypeStruct((t, d), F32),
        compiler_params=_params(("parallel", "arbitrary"), est),
        name="out_proj",
    )(merged, w_out, x)


def _ffn_kernel(h_ref, nw_ref, wg_ref, wu_ref, wo_ref, fw_ref, o_ref, hn_ref, acc_ref, *, final_norm):
    f = pl.program_id(1)

    @pl.when(f == 0)
    def _():
        hn_ref[...] = _rmsnorm(h_ref[...], nw_ref[...]).astype(BF16)
        acc_ref[...] = jnp.zeros_like(acc_ref)

    hn = hn_ref[...]
    tf = wg_ref.shape[1]
    part = tf // FFN_SPLIT
    acc = acc_ref[...]
    for s in range(FFN_SPLIT):
        cols = slice(s * part, (s + 1) * part)
        gate = jnp.dot(hn, wg_ref[:, cols], preferred_element_type=F32)
        up = jnp.dot(hn, wu_ref[:, cols], preferred_element_type=F32)
        act = (_silu(gate) * up).astype(BF16)
        acc = acc + jnp.dot(act, wo_ref[cols, :], preferred_element_type=F32)
    acc_ref[...] = acc

    @pl.when(f == pl.num_programs(1) - 1)
    def _():
        out = h_ref[...] + acc_ref[...]
        o_ref[...] = _rmsnorm(out, fw_ref[...]) if final_norm else out


def _ffn(h, norm_w, w_in, w_out, final_w, final_norm):
    t, d = h.shape
    ff = w_out.shape[0]
    tm = _tile(t, 512)
    tf = _tile(ff, 512)
    nf = ff // tf
    est = 2 * tm * d * 4 * 2 + tm * d * 2 + tm * d * 4 + 2 * 3 * d * tf * 2 + 4 * tm * tf * 4
    return pl.pallas_call(
        functools.partial(_ffn_kernel, final_norm=final_norm),
        grid=(t // tm, nf),
        in_specs=[
            pl.BlockSpec((tm, d), lambda i, f: (i, 0)),
            pl.BlockSpec((1, d), lambda i, f: (0, 0)),
            pl.BlockSpec((d, tf), lambda i, f: (0, f)),
            pl.BlockSpec((d, tf), lambda i, f: (0, nf + f)),
            pl.BlockSpec((tf, d), lambda i, f: (f, 0)),
            pl.BlockSpec((1, d), lambda i, f: (0, 0)),
        ],
        out_specs=pl.BlockSpec((tm, d), lambda i, f: (i, 0)),
        out_shape=jax.ShapeDtypeStruct((t, d), F32),
        scratch_shapes=[pltpu.VMEM((tm, d), BF16), pltpu.VMEM((tm, d), F32)],
        compiler_params=_params(("parallel", "arbitrary"), est),
        name="ffn",
    )(h, norm_w.reshape(1, d), w_in, w_in, w_out, final_w.reshape(1, d))


def _layer_weights(w_in, conv_w, conv_b, dt_bias, a_log, d_skip, ssm_norm_w, d_model):
    d_inner = ssm_norm_w.shape[0]
    n_heads = d_skip.shape[0]
    gn = SSM_GROUPS * STATE_DIM
    conv_dim = d_inner + 2 * gn
    q_dim = d_model
    kv_dim = KV_HEADS * ATTN_HEAD_DIM
    cuts = np.cumsum([d_inner, conv_dim, 2 * n_heads, q_dim, kv_dim, kv_dim, 2 * d_model])
    z_w, xbc_w, dt_w, q_w, k_w, v_w, gate_w = jnp.split(w_in, cuts[:-1], axis=1)
    w_main = jnp.concatenate([z_w, q_w, xbc_w, gate_w, k_w, v_w], axis=1).astype(BF16)
    blocks = {}
    off = 0
    for name, width in (("z", d_inner), ("q", q_dim), ("xbc", conv_dim), ("gate", 2 * d_model), ("k", kv_dim), ("v", kv_dim)):
        blocks[name] = off // COL_BLOCK
        off += width
    n_conv = conv_dim // COL_BLOCK
    conv_w3 = conv_w.reshape(CONV_WIDTH, n_conv, COL_BLOCK).transpose(1, 0, 2)
    conv_b3 = conv_b.reshape(n_conv, 1, COL_BLOCK)
    dtb = dt_bias.reshape(1, 2 * n_heads)
    a_neg = (-jnp.exp(a_log.astype(F32))).reshape(1, 2 * n_heads)
    dskip = jnp.repeat(d_skip.astype(F32), HEAD_DIM_SSM).reshape(SSM_GROUPS, 1, COL_BLOCK)
    nw = ssm_norm_w.astype(F32).reshape(SSM_GROUPS, 1, COL_BLOCK)
    return w_main, dt_w.astype(BF16), blocks, conv_w3, conv_b3, dtb, a_neg, dskip, nw


def _trunk(x, mix_norm_w, w_in, conv_w, conv_b, dt_bias, a_log, d_skip, ssm_norm_w, w_ssm_branch, rel_bias,
           attn_sink, w_attn_branch, w_out, ffn_norm_w, w_ffn_in, w_ffn_out, final_norm_w):
    bsz, seq, d_model = x.shape
    depth = w_in.shape[0]
    h = x.reshape(bsz * seq, d_model)
    bias = _attention_bias(rel_bias).reshape(KV_HEADS, Q_PER_KV * ATTN_BLOCK, 3 * ATTN_BLOCK)
    for l in range(depth):
        w_main, w_dt, blocks, conv_w3, conv_b3, dtb, a_neg, dskip, nw = _layer_weights(
            w_in[l], conv_w[l], conv_b[l], dt_bias[l], a_log[l], d_skip[l], ssm_norm_w[l], d_model)
        proj, dt_raw = _in_proj(h, mix_norm_w[l], w_main, w_dt)
        n_xs = ssm_norm_w.shape[1] // COL_BLOCK
        xs, bm, cm = _conv_silu(proj, conv_w3, conv_b3, bsz, seq, blocks["xbc"], n_xs)
        y_fwd = _ssd_forward(xs, bm, cm, dt_raw, dtb, a_neg, bsz, seq)
        y_ssm = _ssd_backward(xs, bm, cm, dt_raw, dtb, a_neg, y_fwd, proj, dskip, nw, bsz, seq, blocks["z"])
        attn = _attention(proj, bias, attn_sink[l], bsz, seq, blocks["q"], blocks["k"], blocks["v"])
        merged = _merge(y_ssm, attn, proj, w_ssm_branch[l].astype(BF16), w_attn_branch[l].astype(BF16), blocks["gate"])
        h = _out_proj(merged, w_out[l].astype(BF16), h)
        h = _ffn(h, ffn_norm_w[l], w_ffn_in[l].astype(BF16), w_ffn_out[l].astype(BF16), final_norm_w,
                 final_norm=l == depth - 1)
    return h.reshape(bsz, seq, d_model)


def kernel(x_prompt, x_sample, mix_norm_w, w_in, conv_w, conv_b, dt_bias, a_log, d_skip, ssm_norm_w, w_ssm_branch,
           rel_bias, attn_sink, w_attn_branch, w_out, ffn_norm_w, w_ffn_in, w_ffn_out, final_norm_w):
    args = (mix_norm_w, w_in, conv_w, conv_b, dt_bias, a_log, d_skip, ssm_norm_w, w_ssm_branch, rel_bias, attn_sink,
            w_attn_branch, w_out, ffn_norm_w, w_ffn_in, w_ffn_out, final_norm_w)
    return (_trunk(x_prompt, *args), _trunk(x_sample, *args))
```

```python
import functools
import math

import numpy as np
import jax
import jax.numpy as jnp
from jax import lax
from jax.experimental import pallas as pl
from jax.experimental.pallas import tpu as pltpu

F32 = jnp.float32
BF16 = jnp.bfloat16

EPS = 1e-6
HEAD_DIM_SSM = 64
STATE_DIM = 128
SSM_GROUPS = 8
CHUNK = 128
FFN_SPLIT = 2
SSD_STEP_CHUNKS = 4
CONV_WIDTH = 5
ATTN_HEAD_DIM = 128
KV_HEADS = 4
Q_PER_KV = 4
WINDOW = 128
ATTN_BLOCK = 128
REL_BUCKETS = 32
REL_MAX_DIST = 128
COL_BLOCK = 512
SUBLANES = 8
HALO_ROWS = 16
CONV_ROWS = 128
CONV_PAD = 64
LOG2_E = math.log2(math.e)
MASK_VALUE = -1e30
VMEM_CAP = 60000 * 1024


def _vmem_limit(estimate_bytes):
    return int(min(VMEM_CAP, max(estimate_bytes * 5 // 4, 16 * 1024 * 1024)))


def _params(semantics, vmem_estimate):
    return pltpu.CompilerParams(dimension_semantics=semantics,
                                vmem_limit_bytes=_vmem_limit(vmem_estimate))


def _sigmoid(x):
    return 1.0 / (1.0 + jnp.exp2(x * -LOG2_E))


def _silu(x):
    return x * _sigmoid(x)


def _softplus(x):
    return jnp.maximum(x, 0.0) + jnp.log1p(jnp.exp(-jnp.abs(x)))


def _rmsnorm(x, w):
    ms = jnp.mean(x * x, axis=-1, keepdims=True)
    return x * lax.rsqrt(ms + EPS) * w


def _tile(n, want):
    t = min(n, want)
    while n % t:
        t //= 2
    return t


def _inproj_kernel(x_ref, nw_ref, w_ref, wdt_ref, o_ref, dt_ref, xn_ref):
    @pl.when(pl.program_id(1) == 0)
    def _():
        xn = _rmsnorm(x_ref[...], nw_ref[...]).astype(BF16)
        xn_ref[...] = xn
        dt_ref[...] = jnp.dot(xn, wdt_ref[...], preferred_element_type=F32)

    out = jnp.dot(xn_ref[...], w_ref[...], preferred_element_type=F32).astype(o_ref.dtype)
    for c in range(o_ref.shape[0]):
        o_ref[c] = out[:, c * COL_BLOCK:(c + 1) * COL_BLOCK]


def _in_proj(x, norm_w, w_main, w_dt):
    t, d = x.shape
    n = w_main.shape[1]
    nb = n // COL_BLOCK
    tm = _tile(t, 1024)
    per = 2 if nb % 2 == 0 else 1
    tn = per * COL_BLOCK
    ndt = w_dt.shape[1]
    est = 2 * tm * d * 4 + tm * d * 2 + 2 * d * tn * 2 + 2 * tm * tn * 2 + tm * tn * 4 + 2 * d * ndt * 2 + 2 * tm * ndt * 4
    return pl.pallas_call(
        _inproj_kernel,
        grid=(t // tm, nb // per),
        in_specs=[
            pl.BlockSpec((tm, d), lambda i, j: (i, 0)),
            pl.BlockSpec((1, d), lambda i, j: (0, 0)),
            pl.BlockSpec((d, tn), lambda i, j: (0, j)),
            pl.BlockSpec((d, ndt), lambda i, j: (0, 0)),
        ],
        out_specs=[
            pl.BlockSpec((per, tm, COL_BLOCK), lambda i, j: (j, i, 0)),
            pl.BlockSpec((tm, ndt), lambda i, j: (i, 0)),
        ],
        out_shape=[
            jax.ShapeDtypeStruct((nb, t, COL_BLOCK), BF16),
            jax.ShapeDtypeStruct((t, ndt), F32),
        ],
        scratch_shapes=[pltpu.VMEM((tm, d), BF16)],
        compiler_params=_params(("parallel", "arbitrary"), est),
        name="in_proj",
    )(x, norm_w.reshape(1, d), w_main, w_dt)


def _shift_matrix():
    half = CONV_WIDTH // 2
    n_shift = CONV_WIDTH - 1
    rows = np.arange(n_shift * CONV_ROWS)[:, None]
    cols = np.arange(CONV_ROWS + 2 * CONV_PAD)[None, :]
    q, rem = rows // (n_shift * SUBLANES), rows % (n_shift * SUBLANES)
    blk, t = rem // SUBLANES, q * SUBLANES + rem % SUBLANES
    off = blk - half + (blk >= half)
    return jnp.asarray(cols == t + CONV_PAD + off, BF16)


def _conv_kernel(main_ref, prev_ref, next_ref, w_ref, b_ref, shift_ref, xs_ref, bm_ref, cm_ref, win_ref, *, tr, n_xs):
    i = pl.program_id(1)
    last = pl.num_programs(1) - 1
    half = CONV_WIDTH // 2
    n_blocks = main_ref.shape[0]
    shift = shift_ref[...]
    win_ref[0:CONV_PAD - HALO_ROWS, :] = jnp.zeros((CONV_PAD - HALO_ROWS, COL_BLOCK), BF16)
    win_ref[CONV_PAD + tr + HALO_ROWS:, :] = jnp.zeros((CONV_PAD - HALO_ROWS, COL_BLOCK), BF16)

    def conv_block(c):
        halo_zero = jnp.zeros((HALO_ROWS, COL_BLOCK), BF16)
        win_ref[CONV_PAD - HALO_ROWS:CONV_PAD, :] = jnp.where(i > 0, prev_ref[c], halo_zero)
        win_ref[CONV_PAD:CONV_PAD + tr, :] = main_ref[c]
        win_ref[CONV_PAD + tr:CONV_PAD + tr + HALO_ROWS, :] = jnp.where(i < last, next_ref[c], halo_zero)
        w = w_ref[c]
        bias = b_ref[c]
        outs = []
        for r in range(tr // CONV_ROWS):
            xwin = win_ref[r * CONV_ROWS:(r + 1) * CONV_ROWS + 2 * CONV_PAD, :]
            taps = jnp.dot(shift, xwin, preferred_element_type=F32)
            centre = xwin[CONV_PAD:CONV_PAD + CONV_ROWS].astype(F32)
            groups = []
            for q in range(CONV_ROWS // SUBLANES):
                acc = bias + centre[q * SUBLANES:(q + 1) * SUBLANES] * w[half:half + 1]
                for b in range(CONV_WIDTH - 1):
                    j = b + (b >= half)
                    row0 = (q * (CONV_WIDTH - 1) + b) * SUBLANES
                    acc = acc + taps[row0:row0 + SUBLANES] * w[j:j + 1]
                groups.append(acc)
            outs.append(_silu(jnp.concatenate(groups, axis=0)))
        return outs

    def xs_body(c, carry):
        for r, act in enumerate(conv_block(c)):
            xs_ref[c, r * CONV_ROWS:(r + 1) * CONV_ROWS, :] = act.astype(xs_ref.dtype)
        return carry

    lax.fori_loop(0, n_xs, xs_body, 0)
    per = COL_BLOCK // STATE_DIM
    n_b = (n_blocks - n_xs) // 2

    def state_body(first, dst):
        def body(c, carry):
            for r, act in enumerate(conv_block(c)):
                act = act.astype(dst.dtype)
                for k in range(per):
                    dst[(c - first) * per + k, r * CONV_ROWS:(r + 1) * CONV_ROWS, :] = (
                        act[:, k * STATE_DIM:(k + 1) * STATE_DIM])
            return carry
        return body

    lax.fori_loop(n_xs, n_xs + n_b, state_body(n_xs, bm_ref), 0)
    lax.fori_loop(n_xs + n_b, n_blocks, state_body(n_xs + n_b, cm_ref), 0)


def _conv_silu(proj, conv_w, conv_b, bsz, seq, first_block, n_xs):
    t = proj.shape[1]
    n_blocks = conv_w.shape[0]
    tr = _tile(seq, 512)
    nr = seq // tr
    hb = tr // HALO_ROWS
    last_halo = t // HALO_ROWS - 1
    n_bc = (n_blocks - n_xs) // 2 * (COL_BLOCK // STATE_DIM)
    cb0 = first_block // n_blocks
    assert cb0 * n_blocks == first_block
    assert tr % CONV_ROWS == 0
    est = (2 * n_blocks * (tr + 2 * HALO_ROWS) * COL_BLOCK * 2 + 2 * (n_xs * tr * COL_BLOCK + 2 * n_bc * tr * STATE_DIM) * 2
           + (tr + 2 * CONV_PAD) * COL_BLOCK * 2 + 4 * n_blocks * 8 * COL_BLOCK * 4 + 8 * 1024 * 1024)
    shift = _shift_matrix()
    kern = functools.partial(_conv_kernel, tr=tr, n_xs=n_xs)
    return pl.pallas_call(
        kern,
        grid=(bsz, nr),
        in_specs=[
            pl.BlockSpec((n_blocks, tr, COL_BLOCK), lambda b, i: (cb0, b * nr + i, 0)),
            pl.BlockSpec((n_blocks, HALO_ROWS, COL_BLOCK),
                         lambda b, i: (cb0, jnp.maximum((b * nr + i) * hb - 1, 0), 0)),
            pl.BlockSpec((n_blocks, HALO_ROWS, COL_BLOCK),
                         lambda b, i: (cb0, jnp.minimum((b * nr + i + 1) * hb, last_halo), 0)),
            pl.BlockSpec((n_blocks, CONV_WIDTH, COL_BLOCK), lambda b, i: (0, 0, 0)),
            pl.BlockSpec((n_blocks, 1, COL_BLOCK), lambda b, i: (0, 0, 0)),
            pl.BlockSpec(shift.shape, lambda b, i: (0, 0)),
        ],
        out_specs=[
            pl.BlockSpec((n_xs, tr, COL_BLOCK), lambda b, i: (0, b * nr + i, 0)),
            pl.BlockSpec((n_bc, tr, STATE_DIM), lambda b, i: (0, b * nr + i, 0)),
            pl.BlockSpec((n_bc, tr, STATE_DIM), lambda b, i: (0, b * nr + i, 0)),
        ],
        out_shape=[
            jax.ShapeDtypeStruct((n_xs, t, COL_BLOCK), BF16),
            jax.ShapeDtypeStruct((n_bc, t, STATE_DIM), BF16),
            jax.ShapeDtypeStruct((n_bc, t, STATE_DIM), BF16),
        ],
        scratch_shapes=[pltpu.VMEM((tr + 2 * CONV_PAD, COL_BLOCK), BF16)],
        compiler_params=_params(("parallel", "parallel"), est),
        name="conv_silu",
    )(proj, proj, proj, conv_w, conv_b, shift)


def _scan_rows(v, reverse):
    n = v.shape[0]
    row = lax.broadcasted_iota(jnp.int32, v.shape, 0)
    k = 1
    while k < n:
        if reverse:
            shifted = pltpu.roll(v, n - k, axis=0)
            keep = row < n - k
        else:
            shifted = pltpu.roll(v, k, axis=0)
            keep = row >= k
        v = v + jnp.where(keep, shifted, 0.0)
        k *= 2
    return v


def _ssd_chunk_prologue(dt_raw, dtb_ref, a_ref, a_sc, at_sc, wdt_sc, reverse):
    dt = _softplus(dt_raw + dtb_ref[...])
    acc = _scan_rows(dt * a_ref[...], reverse)
    a_sc[...] = acc * LOG2_E
    acc_t = acc.T
    dt_t = dt.T
    at_sc[...] = acc_t * LOG2_E - jnp.log2(dt_t)
    n = acc.shape[0]
    total = acc_t[:, 0:1] if reverse else acc_t[:, n - 1:n]
    wdt_sc[...] = jnp.exp(total - acc_t) * dt_t


def _ssd_group(g, rows, xs_ref, bm_ref, cm_ref, st_ref, a_sc, at_sc, wdt_sc, head0, reverse):
    l = CHUNK
    xs = xs_ref[g, rows, :]
    bm = bm_ref[g, rows, :]
    cm = cm_ref[g, rows, :]
    cb = lax.dot_general(cm, bm, (((1,), (1,)), ((), ())), preferred_element_type=F32).astype(BF16)
    bm_t = bm.astype(F32).T.astype(BF16)

    heads_per_group = COL_BLOCK // HEAD_DIM_SSM
    first = head0 + g * heads_per_group
    a_all = a_sc[...]
    row_i = lax.broadcasted_iota(jnp.int32, (l, l), 0)
    col_i = lax.broadcasted_iota(jnp.int32, (l, l), 1)
    causal = (row_i <= col_i) if reverse else (row_i >= col_i)
    lane = lax.broadcasted_iota(jnp.int32, (l, 2 * HEAD_DIM_SSM), 1)
    low_half = lane < HEAD_DIM_SSM
    keep_low = jnp.where(low_half, 1.0, 0.0).astype(BF16)
    keep_high = jnp.where(low_half, 0.0, 1.0).astype(BF16)
    edge = 0 if reverse else l - 1

    y_slabs = []
    for pair in range(heads_per_group // 2):
        sl = slice(pair * 2 * HEAD_DIM_SSM, (pair + 1) * 2 * HEAD_DIM_SSM)
        st_pair = st_ref[g, :, sl]
        y_off = jnp.dot(cm, st_pair.astype(BF16), preferred_element_type=F32)
        xs_pair = xs[:, sl]
        rhs = jnp.concatenate([xs_pair * keep_low, xs_pair * keep_high], axis=0)
        m_parts, bw_parts, a_col_parts, a_tot_parts = [], [], [], []
        for sub in range(2):
            h = first + pair * 2 + sub
            a_col = a_all[:, h:h + 1]
            a_row = at_sc[h:h + 1, :]
            wd_row = wdt_sc[h:h + 1, :]
            seg = jnp.where(causal, a_col - a_row, -jnp.inf)
            m_parts.append(cb * jnp.exp2(seg).astype(BF16))
            bw_parts.append(bm_t * wd_row.astype(BF16))
            a_col_parts.append(a_col)
            a_tot_parts.append(a_all[edge:edge + 1, h:h + 1])
        m_pair = jnp.concatenate(m_parts, axis=1)
        bw_pair = jnp.concatenate(bw_parts, axis=1)
        a_col_pair = jnp.where(low_half, a_col_parts[0], a_col_parts[1])
        a_tot_pair = jnp.where(low_half[0:1], a_tot_parts[0], a_tot_parts[1])
        y_slabs.append(jnp.dot(m_pair, rhs, preferred_element_type=F32) + jnp.exp2(a_col_pair) * y_off)
        st_ref[g, :, sl] = jnp.exp2(a_tot_pair) * st_pair + jnp.dot(bw_pair, rhs, preferred_element_type=F32)
    return jnp.concatenate(y_slabs, axis=1), xs


def _chunk_rows(index):
    return pl.ds(pl.multiple_of(index * CHUNK, CHUNK), CHUNK)


def _ssd_fwd_kernel(xs_ref, bm_ref, cm_ref, dt_ref, dtb_ref, a_ref, y_ref, st_ref, a_sc, at_sc, wdt_sc):
    @pl.when(pl.program_id(1) == 0)
    def _():
        st_ref[...] = jnp.zeros_like(st_ref)

    def chunk(c, carry):
        rows = _chunk_rows(c)
        _ssd_chunk_prologue(dt_ref[rows, :], dtb_ref, a_ref, a_sc, at_sc, wdt_sc, reverse=False)
        for g in range(SSM_GROUPS):
            y, _ = _ssd_group(g, rows, xs_ref, bm_ref, cm_ref, st_ref, a_sc, at_sc, wdt_sc, head0=0, reverse=False)
            y_ref[g, rows, :] = y.astype(y_ref.dtype)
        return carry

    lax.fori_loop(0, dt_ref.shape[0] // CHUNK, chunk, 0)


def _ssd_bwd_kernel(xs_ref, bm_ref, cm_ref, dt_ref, dtb_ref, a_ref, yf_ref, z_ref, dskip_ref, nw_ref, y_ref,
                    st_ref, a_sc, at_sc, wdt_sc, *, n_heads):
    @pl.when(pl.program_id(1) == 0)
    def _():
        st_ref[...] = jnp.zeros_like(st_ref)

    n_chunks = dt_ref.shape[0] // CHUNK

    def chunk(it, carry):
        rows = _chunk_rows(n_chunks - 1 - it)
        _ssd_chunk_prologue(dt_ref[rows, :], dtb_ref, a_ref, a_sc, at_sc, wdt_sc, reverse=True)
        for g in range(SSM_GROUPS):
            y, xs = _ssd_group(g, rows, xs_ref, bm_ref, cm_ref, st_ref, a_sc, at_sc, wdt_sc, head0=n_heads,
                               reverse=True)
            y = y + yf_ref[g, rows, :].astype(F32) + xs.astype(F32) * dskip_ref[g]
            y = y * _silu(z_ref[g, rows, :].astype(F32))
            y_ref[g, rows, :] = _rmsnorm(y, nw_ref[g]).astype(y_ref.dtype)
        return carry

    lax.fori_loop(0, n_chunks, chunk, 0)


def _ssd_scratch(two_h):
    return [
        pltpu.VMEM((SSM_GROUPS, STATE_DIM, COL_BLOCK), F32),
        pltpu.VMEM((CHUNK, two_h), F32),
        pltpu.VMEM((two_h, CHUNK), F32),
        pltpu.VMEM((two_h, CHUNK), F32),
    ]


def _ssd_est(two_h, n_slabs, rows):
    return (2 * n_slabs * SSM_GROUPS * rows * COL_BLOCK * 2 + 4 * SSM_GROUPS * rows * STATE_DIM * 2
            + SSM_GROUPS * STATE_DIM * COL_BLOCK * 4 + 2 * rows * two_h * 4 + 6 * CHUNK * two_h * 4 + 8 * 1024 * 1024)


def _ssd_forward(xs, bm, cm, dt_raw, dt_bias, a_neg, bsz, seq):
    g, t, _ = xs.shape
    rows = _tile(seq, SSD_STEP_CHUNKS * CHUNK)
    nc = seq // rows
    two_h = dt_raw.shape[1]
    slab = pl.BlockSpec((g, rows, COL_BLOCK), lambda b, c: (0, b * nc + c, 0))
    bc = pl.BlockSpec((g, rows, STATE_DIM), lambda b, c: (0, b * nc + c, 0))
    row = pl.BlockSpec((1, two_h), lambda b, c: (0, 0))
    return pl.pallas_call(
        _ssd_fwd_kernel,
        grid=(bsz, nc),
        in_specs=[slab, bc, bc, pl.BlockSpec((rows, two_h), lambda b, c: (b * nc + c, 0)), row, row],
        out_specs=slab,
        out_shape=jax.ShapeDtypeStruct((g, t, COL_BLOCK), BF16),
        scratch_shapes=_ssd_scratch(two_h),
        compiler_params=_params(("parallel", "arbitrary"), _ssd_est(two_h, 2, rows)),
        name="ssd_fwd",
    )(xs, bm, cm, dt_raw, dt_bias, a_neg)


def _ssd_backward(xs, bm, cm, dt_raw, dt_bias, a_neg, y_fwd, proj, d_skip, norm_w, bsz, seq, z_block):
    g, t, _ = xs.shape
    rows = _tile(seq, SSD_STEP_CHUNKS * CHUNK)
    nc = seq // rows
    two_h = dt_raw.shape[1]
    rev = lambda b, c: (0, b * nc + nc - 1 - c, 0)
    slab = pl.BlockSpec((g, rows, COL_BLOCK), rev)
    bc = pl.BlockSpec((g, rows, STATE_DIM), rev)
    row = pl.BlockSpec((1, two_h), lambda b, c: (0, 0))
    par = pl.BlockSpec((g, 1, COL_BLOCK), lambda b, c: (0, 0, 0))
    zb = z_block // g
    assert zb * g == z_block
    kern = functools.partial(_ssd_bwd_kernel, n_heads=two_h // 2)
    return pl.pallas_call(
        kern,
        grid=(bsz, nc),
        in_specs=[slab, bc, bc, pl.BlockSpec((rows, two_h), lambda b, c: (b * nc + nc - 1 - c, 0)), row, row,
                  slab, pl.BlockSpec((g, rows, COL_BLOCK), lambda b, c: (zb, b * nc + nc - 1 - c, 0)), par, par],
        out_specs=slab,
        out_shape=jax.ShapeDtypeStruct((g, t, COL_BLOCK), BF16),
        scratch_shapes=_ssd_scratch(two_h),
        compiler_params=_params(("parallel", "arbitrary"), _ssd_est(two_h, 4, rows)),
        name="ssd_bwd",
    )(xs, bm, cm, dt_raw, dt_bias, a_neg, y_fwd, proj, d_skip, norm_w)


def _t5_buckets(rel):
    half = REL_BUCKETS // 2
    ret = (rel > 0).astype(np.int32) * half
    n = np.abs(rel)
    max_exact = half // 2
    large = max_exact + (np.log(np.maximum(n, 1) / max_exact) / np.log(REL_MAX_DIST / max_exact)
                         * (half - max_exact)).astype(np.int32)
    large = np.minimum(large, half - 1)
    return ret + np.where(n < max_exact, n, large).astype(np.int32)


def _bias_kernel(rel_bias_ref, bucket_ref, o_ref):
    h = pl.program_id(0)
    bucket = bucket_ref[...]
    acc = jnp.zeros(bucket.shape, F32)
    for b in range(REL_BUCKETS):
        acc = jnp.where(bucket == b, rel_bias_ref[b, h], acc)
    o_ref[...] = jnp.where(bucket < 0, MASK_VALUE, acc * LOG2_E)


def _attention_bias(rel_bias):
    blk = ATTN_BLOCK
    rel = np.arange(3 * blk)[None, :] - blk - np.arange(blk)[:, None]
    bucket = np.where(np.abs(rel) <= WINDOW, _t5_buckets(rel), -1).astype(np.int32)
    heads = rel_bias.shape[1]
    return pl.pallas_call(
        _bias_kernel,
        grid=(heads,),
        in_specs=[pl.BlockSpec(memory_space=pltpu.SMEM), pl.BlockSpec((blk, 3 * blk), lambda h: (0, 0))],
        out_specs=pl.BlockSpec((None, blk, 3 * blk), lambda h: (h, 0, 0)),
        out_shape=jax.ShapeDtypeStruct((heads, blk, 3 * blk), F32),
        compiler_params=_params(("arbitrary",), 4 * 1024 * 1024),
        name="attn_bias",
    )(rel_bias, jnp.asarray(bucket))


def _attn_kernel(sink_ref, q_ref, k_ref, kp_ref, kn_ref, v_ref, vp_ref, vn_ref, bias_ref, o_ref, kw_ref, vw_ref,
                 *, tq, seq, scale):
    blk = ATTN_BLOCK
    d = ATTN_HEAD_DIM
    kw_ref[0:blk, :] = kp_ref[...]
    kw_ref[blk:blk + tq, :] = k_ref[...]
    kw_ref[blk + tq:, :] = kn_ref[...]
    vw_ref[0:blk, :] = vp_ref[...]
    vw_ref[blk:blk + tq, :] = v_ref[...]
    vw_ref[blk + tq:, :] = vn_ref[...]
    step0 = pl.program_id(1) * tq
    key_off = lax.broadcasted_iota(jnp.int32, (1, 3 * blk), 1) - blk
    n_sub = tq // blk
    for g in range(KV_HEADS):
        for i in range(n_sub):
            kwin = kw_ref[i * blk:(i + 3) * blk, g * d:(g + 1) * d]
            vwin = vw_ref[i * blk:(i + 3) * blk, g * d:(g + 1) * d]
            q_rows = q_ref[g, i * blk:(i + 1) * blk, :]
            q4 = jnp.concatenate([q_rows[:, r * d:(r + 1) * d] for r in range(Q_PER_KV)], axis=0)
            logits = lax.dot_general(q4, kwin, (((1,), (1,)), ((), ())), preferred_element_type=F32)
            logits = logits * (scale * LOG2_E) + bias_ref[g]
            if i == 0 or i == n_sub - 1:
                key_pos = step0 + i * blk + key_off
                logits = jnp.where((key_pos >= 0) & (key_pos < seq), logits, MASK_VALUE)
            probs, inv = [], []
            for r in range(Q_PER_KV):
                lg = logits[r * blk:(r + 1) * blk]
                sink = sink_ref[g * Q_PER_KV + r] * LOG2_E
                m = jnp.maximum(jnp.max(lg, axis=-1, keepdims=True), sink)
                p = jnp.exp2(lg - m)
                inv.append(1.0 / (jnp.sum(p, axis=-1, keepdims=True) + jnp.exp2(sink - m)))
                probs.append(p.astype(BF16))
            out = jnp.dot(jnp.concatenate(probs, axis=0), vwin, preferred_element_type=F32)
            for r in range(Q_PER_KV):
                o_ref[g, i * blk:(i + 1) * blk, r * d:(r + 1) * d] = (
                    out[r * blk:(r + 1) * blk] * inv[r]).astype(o_ref.dtype)


def _attention(proj, bias, sink, bsz, seq, q_block, k_block, v_block):
    t = proj.shape[1]
    blk = ATTN_BLOCK
    tq = _tile(seq, 512)
    nq = seq // tq
    per = tq // blk
    last = t // blk - 1
    width = Q_PER_KV * ATTN_HEAD_DIM
    assert width == COL_BLOCK and KV_HEADS * ATTN_HEAD_DIM == COL_BLOCK
    qb = q_block // KV_HEADS
    assert qb * KV_HEADS == q_block
    main = lambda cb: pl.BlockSpec((None, tq, COL_BLOCK), lambda b, i: (cb, b * nq + i, 0))
    prev = lambda cb: pl.BlockSpec((None, blk, COL_BLOCK), lambda b, i: (cb, jnp.maximum((b * nq + i) * per - 1, 0), 0))
    nxt = lambda cb: pl.BlockSpec((None, blk, COL_BLOCK), lambda b, i: (cb, jnp.minimum((b * nq + i + 1) * per, last), 0))
    kern = functools.partial(_attn_kernel, tq=tq, seq=seq, scale=ATTN_HEAD_DIM ** -0.5)
    est = (2 * KV_HEADS * tq * COL_BLOCK * 2 * 2 + 4 * (tq + 2 * blk) * COL_BLOCK * 2 + 2 * (tq + 2 * blk) * COL_BLOCK * 2
           + 2 * KV_HEADS * Q_PER_KV * blk * 3 * blk * 4 + 8 * 1024 * 1024)
    window = pltpu.VMEM((tq + 2 * blk, COL_BLOCK), BF16)
    return pl.pallas_call(
        kern,
        grid=(bsz, nq),
        in_specs=[
            pl.BlockSpec(memory_space=pltpu.SMEM),
            pl.BlockSpec((KV_HEADS, tq, COL_BLOCK), lambda b, i: (qb, b * nq + i, 0)),
            main(k_block), prev(k_block), nxt(k_block),
            main(v_block), prev(v_block), nxt(v_block),
            pl.BlockSpec((KV_HEADS, Q_PER_KV * blk, 3 * blk), lambda b, i: (0, 0, 0)),
        ],
        out_specs=pl.BlockSpec((KV_HEADS, tq, COL_BLOCK), lambda b, i: (0, b * nq + i, 0)),
        out_shape=jax.ShapeDtypeStruct((KV_HEADS, t, COL_BLOCK), BF16),
        scratch_shapes=[window, window],
        compiler_params=_params(("parallel", "parallel"), est),
        name="attention",
    )(sink, proj, proj, proj, proj, proj, proj, proj, bias)


def _merge_kernel(ys_ref, at_ref, ga_ref, gb_ref, ws_ref, wa_ref, o_ref):
    def branch(lhs_ref, w_ref):
        acc = None
        for c in range(lhs_ref.shape[0]):
            part = jnp.dot(lhs_ref[c], w_ref[c * COL_BLOCK:(c + 1) * COL_BLOCK, :], preferred_element_type=F32)
            acc = part if acc is None else acc + part
        return acc

    ya = branch(ys_ref, ws_ref)
    yb = branch(at_ref, wa_ref)
    merged = _sigmoid(ga_ref[...].astype(F32)) * ya + _sigmoid(gb_ref[...].astype(F32)) * yb
    o_ref[...] = merged.astype(o_ref.dtype)


def _merge(y_ssm, attn, proj, w_ssm, w_attn, gate_block):
    ks, t, _ = y_ssm.shape
    ka = attn.shape[0]
    dm = w_ssm.shape[1]
    nb = dm // COL_BLOCK
    tm = _tile(t, 1024)
    est = 2 * (ks + ka) * tm * COL_BLOCK * 2 + 2 * (ks + ka) * COL_BLOCK * COL_BLOCK * 2 + 6 * tm * COL_BLOCK * 2 + 3 * tm * COL_BLOCK * 4
    return pl.pallas_call(
        _merge_kernel,
        grid=(t // tm, nb),
        in_specs=[
            pl.BlockSpec((ks, tm, COL_BLOCK), lambda i, j: (0, i, 0)),
            pl.BlockSpec((ka, tm, COL_BLOCK), lambda i, j: (0, i, 0)),
            pl.BlockSpec((None, tm, COL_BLOCK), lambda i, j: (gate_block + j, i, 0)),
            pl.BlockSpec((None, tm, COL_BLOCK), lambda i, j: (gate_block + nb + j, i, 0)),
            pl.BlockSpec((ks * COL_BLOCK, COL_BLOCK), lambda i, j: (0, j)),
            pl.BlockSpec((ka * COL_BLOCK, COL_BLOCK), lambda i, j: (0, j)),
        ],
        out_specs=pl.BlockSpec((None, tm, COL_BLOCK), lambda i, j: (j, i, 0)),
        out_shape=jax.ShapeDtypeStruct((nb, t, COL_BLOCK), BF16),
        compiler_params=_params(("parallel", "arbitrary"), est),
        name="merge",
    )(y_ssm, attn, proj, proj, w_ssm, w_attn)


def _outproj_kernel(m_ref, w_ref, x_ref, o_ref):
    acc = x_ref[...]
    for c in range(m_ref.shape[0]):
        acc = acc + jnp.dot(m_ref[c], w_ref[c * COL_BLOCK:(c + 1) * COL_BLOCK, :], preferred_element_type=F32)
    o_ref[...] = acc


def _out_proj(merged, w_out, x):
    kb, t, _ = merged.shape
    d = w_out.shape[1]
    tm = _tile(t, 512)
    tn = d
    est = 2 * kb * tm * COL_BLOCK * 2 + 2 * kb * COL_BLOCK * tn * 2 + 4 * tm * tn * 4 + tm * tn * 4
    return pl.pallas_call(
        _outproj_kernel,
        grid=(t // tm, d // tn),
        in_specs=[
            pl.BlockSpec((kb, tm, COL_BLOCK), lambda i, j: (0, i, 0)),
            pl.BlockSpec((kb * COL_BLOCK, tn), lambda i, j: (0, j)),
            pl.BlockSpec((tm, tn), lambda i, j: (i, j)),
        ],
        out_specs=pl.BlockSpec((tm, tn), lambda i, j: (i, j)),
        out_shape=jax.ShapeDtypeStruct((t, d), F32),
        compiler_params=_params(("parallel", "arbitrary"), est),
        name="out_proj",
    )(merged, w_out, x)


def _ffn_kernel(h_ref, nw_ref, wg_ref, wu_ref, wo_ref, fw_ref, o_ref, hn_ref, acc_ref, *, final_norm):
    f = pl.program_id(1)

    @pl.when(f == 0)
    def _():
        hn_ref[...] = _rmsnorm(h_ref[...], nw_ref[...]).astype(BF16)
        acc_ref[...] = jnp.zeros_like(acc_ref)

    hn = hn_ref[...]
    tf = wg_ref.shape[1]
    part = tf // FFN_SPLIT
    acc = acc_ref[...]
    for s in range(FFN_SPLIT):
        cols = slice(s * part, (s + 1) * part)
        gate = jnp.dot(hn, wg_ref[:, cols], preferred_element_type=F32)
        up = jnp.dot(hn, wu_ref[:, cols], preferred_element_type=F32)
        act = (_silu(gate) * up).astype(BF16)
        acc = acc + jnp.dot(act, wo_ref[cols, :], preferred_element_type=F32)
    acc_ref[...] = acc

    @pl.when(f == pl.num_programs(1) - 1)
    def _():
        out = h_ref[...] + acc_ref[...]
        o_ref[...] = _rmsnorm(out, fw_ref[...]) if final_norm else out


def _ffn(h, norm_w, w_in, w_out, final_w, final_norm):
    t, d = h.shape
    ff = w_out.shape[0]
    tm = _tile(t, 512)
    tf = _tile(ff, 512)
    nf = ff // tf
    est = 2 * tm * d * 4 * 2 + tm * d * 2 + tm * d * 4 + 2 * 3 * d * tf * 2 + 4 * tm * tf * 4
    return pl.pallas_call(
        functools.partial(_ffn_kernel, final_norm=final_norm),
        grid=(t // tm, nf),
        in_specs=[
            pl.BlockSpec((tm, d), lambda i, f: (i, 0)),
            pl.BlockSpec((1, d), lambda i, f: (0, 0)),
            pl.BlockSpec((d, tf), lambda i, f: (0, f)),
            pl.BlockSpec((d, tf), lambda i, f: (0, nf + f)),
            pl.BlockSpec((tf, d), lambda i, f: (f, 0)),
            pl.BlockSpec((1, d), lambda i, f: (0, 0)),
        ],
        out_specs=pl.BlockSpec((tm, d), lambda i, f: (i, 0)),
        out_shape=jax.ShapeDtypeStruct((t, d), F32),
        scratch_shapes=[pltpu.VMEM((tm, d), BF16), pltpu.VMEM((tm, d), F32)],
        compiler_params=_params(("parallel", "arbitrary"), est),
        name="ffn",
    )(h, norm_w.reshape(1, d), w_in, w_in, w_out, final_w.reshape(1, d))


def _layer_weights(w_in, conv_w, conv_b, dt_bias, a_log, d_skip, ssm_norm_w, d_model):
    d_inner = ssm_norm_w.shape[0]
    n_heads = d_skip.shape[0]
    gn = SSM_GROUPS * STATE_DIM
    conv_dim = d_inner + 2 * gn
    q_dim = d_model
    kv_dim = KV_HEADS * ATTN_HEAD_DIM
    cuts = np.cumsum([d_inner, conv_dim, 2 * n_heads, q_dim, kv_dim, kv_dim, 2 * d_model])
    z_w, xbc_w, dt_w, q_w, k_w, v_w, gate_w = jnp.split(w_in, cuts[:-1], axis=1)
    w_main = jnp.concatenate([z_w, q_w, xbc_w, gate_w, k_w, v_w], axis=1).astype(BF16)
    blocks = {}
    off = 0
    for name, width in (("z", d_inner), ("q", q_dim), ("xbc", conv_dim), ("gate", 2 * d_model), ("k", kv_dim), ("v", kv_dim)):
        blocks[name] = off // COL_BLOCK
        off += width
    n_conv = conv_dim // COL_BLOCK
    conv_w3 = conv_w.reshape(CONV_WIDTH, n_conv, COL_BLOCK).transpose(1, 0, 2)
    conv_b3 = conv_b.reshape(n_conv, 1, COL_BLOCK)
    dtb = dt_bias.reshape(1, 2 * n_heads)
    a_neg = (-jnp.exp(a_log.astype(F32))).reshape(1, 2 * n_heads)
    dskip = jnp.repeat(d_skip.astype(F32), HEAD_DIM_SSM).reshape(SSM_GROUPS, 1, COL_BLOCK)
    nw = ssm_norm_w.astype(F32).reshape(SSM_GROUPS, 1, COL_BLOCK)
    return w_main, dt_w.astype(BF16), blocks, conv_w3, conv_b3, dtb, a_neg, dskip, nw


def _trunk(x, mix_norm_w, w_in, conv_w, conv_b, dt_bias, a_log, d_skip, ssm_norm_w, w_ssm_branch, rel_bias,
           attn_sink, w_attn_branch, w_out, ffn_norm_w, w_ffn_in, w_ffn_out, final_norm_w):
    bsz, seq, d_model = x.shape
    depth = w_in.shape[0]
    h = x.reshape(bsz * seq, d_model)
    bias = _attention_bias(rel_bias).reshape(KV_HEADS, Q_PER_KV * ATTN_BLOCK, 3 * ATTN_BLOCK)
    for l in range(depth):
        w_main, w_dt, blocks, conv_w3, conv_b3, dtb, a_neg, dskip, nw = _layer_weights(
            w_in[l], conv_w[l], conv_b[l], dt_bias[l], a_log[l], d_skip[l], ssm_norm_w[l], d_model)
        proj, dt_raw = _in_proj(h, mix_norm_w[l], w_main, w_dt)
        n_xs = ssm_norm_w.shape[1] // COL_BLOCK
        xs, bm, cm = _conv_silu(proj, conv_w3, conv_b3, bsz, seq, blocks["xbc"], n_xs)
        y_fwd = _ssd_forward(xs, bm, cm, dt_raw, dtb, a_neg, bsz, seq)
        y_ssm = _ssd_backward(xs, bm, cm, dt_raw, dtb, a_neg, y_fwd, proj, dskip, nw, bsz, seq, blocks["z"])
        attn = _attention(proj, bias, attn_sink[l], bsz, seq, blocks["q"], blocks["k"], blocks["v"])
        merged = _merge(y_ssm, attn, proj, w_ssm_branch[l].astype(BF16), w_attn_branch[l].astype(BF16), blocks["gate"])
        h = _out_proj(merged, w_out[l].astype(BF16), h)
        h = _ffn(h, ffn_norm_w[l], w_ffn_in[l].astype(BF16), w_ffn_out[l].astype(BF16), final_norm_w,
                 final_norm=l == depth - 1)
    return h.reshape(bsz, seq, d_model)


def kernel(x_prompt, x_sample, mix_norm_w, w_in, conv_w, conv_b, dt_bias, a_log, d_skip, ssm_norm_w, w_ssm_branch,
           rel_bias, attn_sink, w_attn_branch, w_out, ffn_norm_w, w_ffn_in, w_ffn_out, final_norm_w):
    args = (mix_norm_w, w_in, conv_w, conv_b, dt_bias, a_log, d_skip, ssm_norm_w, w_ssm_branch, rel_bias, attn_sink,
            w_attn_branch, w_out, ffn_norm_w, w_ffn_in, w_ffn_out, final_norm_w)
    return (_trunk(x_prompt, *args), _trunk(x_sample, *args))
```

```python
import functools
import math

import numpy as np
import jax
import jax.numpy as jnp
from jax import lax
from jax.experimental import pallas as pl
from jax.experimental.pallas import tpu as pltpu

F32 = jnp.float32
BF16 = jnp.bfloat16

EPS = 1e-6
HEAD_DIM_SSM = 64
STATE_DIM = 128
SSM_GROUPS = 8
CHUNK = 128
FFN_SPLIT = 2
SSD_STEP_CHUNKS = 4
CONV_WIDTH = 5
ATTN_HEAD_DIM = 128
KV_HEADS = 4
Q_PER_KV = 4
WINDOW = 128
ATTN_BLOCK = 128
REL_BUCKETS = 32
REL_MAX_DIST = 128
COL_BLOCK = 512
SUBLANES = 8
HALO_ROWS = 16
CONV_ROWS = 128
CONV_PAD = 64
LOG2_E = math.log2(math.e)
MASK_VALUE = -1e30
VMEM_CAP = 60000 * 1024


def _vmem_limit(estimate_bytes):
    return int(min(VMEM_CAP, max(estimate_bytes * 5 // 4, 16 * 1024 * 1024)))


def _params(semantics, vmem_estimate):
    return pltpu.CompilerParams(dimension_semantics=semantics,
                                vmem_limit_bytes=_vmem_limit(vmem_estimate))


def _sigmoid(x):
    return 1.0 / (1.0 + jnp.exp2(x * -LOG2_E))


def _silu(x):
    return x * _sigmoid(x)


def _softplus(x):
    return jnp.maximum(x, 0.0) + jnp.log1p(jnp.exp(-jnp.abs(x)))


def _rmsnorm(x, w):
    ms = jnp.mean(x * x, axis=-1, keepdims=True)
    return x * lax.rsqrt(ms + EPS) * w


def _tile(n, want):
    t = min(n, want)
    while n % t:
        t //= 2
    return t


def _serpentine(i, j, n):
    return jnp.where(i % 2 == 0, j, n - 1 - j)


def _inproj_kernel(x_ref, nw_ref, w_ref, wdt_ref, o_ref, dt_ref, xn_ref):
    @pl.when(pl.program_id(1) == 0)
    def _():
        xn = _rmsnorm(x_ref[...], nw_ref[...]).astype(BF16)
        xn_ref[...] = xn
        dt_ref[...] = jnp.dot(xn, wdt_ref[...], preferred_element_type=F32)

    out = jnp.dot(xn_ref[...], w_ref[...], preferred_element_type=F32).astype(o_ref.dtype)
    for c in range(o_ref.shape[0]):
        o_ref[c] = out[:, c * COL_BLOCK:(c + 1) * COL_BLOCK]


def _in_proj(x, norm_w, w_main, w_dt):
    t, d = x.shape
    n = w_main.shape[1]
    nb = n // COL_BLOCK
    tm = _tile(t, 1024)
    per = 2 if nb % 2 == 0 else 1
    tn = per * COL_BLOCK
    ndt = w_dt.shape[1]
    nj = nb // per
    est = 2 * tm * d * 4 + tm * d * 2 + 2 * d * tn * 2 + 2 * tm * tn * 2 + tm * tn * 4 + 2 * d * ndt * 2 + 2 * tm * ndt * 4
    return pl.pallas_call(
        _inproj_kernel,
        grid=(t // tm, nj),
        in_specs=[
            pl.BlockSpec((tm, d), lambda i, j: (i, 0)),
            pl.BlockSpec((1, d), lambda i, j: (0, 0)),
            pl.BlockSpec((d, tn), lambda i, j: (0, _serpentine(i, j, nj))),
            pl.BlockSpec((d, ndt), lambda i, j: (0, 0)),
        ],
        out_specs=[
            pl.BlockSpec((per, tm, COL_BLOCK), lambda i, j: (_serpentine(i, j, nj), i, 0)),
            pl.BlockSpec((tm, ndt), lambda i, j: (i, 0)),
        ],
        out_shape=[
            jax.ShapeDtypeStruct((nb, t, COL_BLOCK), BF16),
            jax.ShapeDtypeStruct((t, ndt), F32),
        ],
        scratch_shapes=[pltpu.VMEM((tm, d), BF16)],
        compiler_params=_params(("parallel", "arbitrary"), est),
        name="in_proj",
    )(x, norm_w.reshape(1, d), w_main, w_dt)


def _shift_matrix():
    half = CONV_WIDTH // 2
    n_shift = CONV_WIDTH - 1
    rows = np.arange(n_shift * CONV_ROWS)[:, None]
    cols = np.arange(CONV_ROWS + 2 * CONV_PAD)[None, :]
    q, rem = rows // (n_shift * SUBLANES), rows % (n_shift * SUBLANES)
    blk, t = rem // SUBLANES, q * SUBLANES + rem % SUBLANES
    off = blk - half + (blk >= half)
    return jnp.asarray(cols == t + CONV_PAD + off, BF16)


def _conv_kernel(main_ref, prev_ref, next_ref, w_ref, b_ref, shift_ref, xs_ref, bm_ref, cm_ref, win_ref, *, tr, n_xs):
    i = pl.program_id(1)
    last = pl.num_programs(1) - 1
    half = CONV_WIDTH // 2
    n_blocks = main_ref.shape[0]
    shift = shift_ref[...]
    win_ref[0:CONV_PAD - HALO_ROWS, :] = jnp.zeros((CONV_PAD - HALO_ROWS, COL_BLOCK), BF16)
    win_ref[CONV_PAD + tr + HALO_ROWS:, :] = jnp.zeros((CONV_PAD - HALO_ROWS, COL_BLOCK), BF16)

    def conv_block(c):
        halo_zero = jnp.zeros((HALO_ROWS, COL_BLOCK), BF16)
        win_ref[CONV_PAD - HALO_ROWS:CONV_PAD, :] = jnp.where(i > 0, prev_ref[c], halo_zero)
        win_ref[CONV_PAD:CONV_PAD + tr, :] = main_ref[c]
        win_ref[CONV_PAD + tr:CONV_PAD + tr + HALO_ROWS, :] = jnp.where(i < last, next_ref[c], halo_zero)
        w = w_ref[c]
        bias = b_ref[c]
        outs = []
        for r in range(tr // CONV_ROWS):
            xwin = win_ref[r * CONV_ROWS:(r + 1) * CONV_ROWS + 2 * CONV_PAD, :]
            taps = jnp.dot(shift, xwin, preferred_element_type=F32)
            centre = xwin[CONV_PAD:CONV_PAD + CONV_ROWS].astype(F32)
            groups = []
            for q in range(CONV_ROWS // SUBLANES):
                acc = bias + centre[q * SUBLANES:(q + 1) * SUBLANES] * w[half:half + 1]
                for b in range(CONV_WIDTH - 1):
                    j = b + (b >= half)
                    row0 = (q * (CONV_WIDTH - 1) + b) * SUBLANES
                    acc = acc + taps[row0:row0 + SUBLANES] * w[j:j + 1]
                groups.append(acc)
            outs.append(_silu(jnp.concatenate(groups, axis=0)))
        return outs

    def xs_body(c, carry):
        for r, act in enumerate(conv_block(c)):
            xs_ref[c, r * CONV_ROWS:(r + 1) * CONV_ROWS, :] = act.astype(xs_ref.dtype)
        return carry

    lax.fori_loop(0, n_xs, xs_body, 0)
    per = COL_BLOCK // STATE_DIM
    n_b = (n_blocks - n_xs) // 2

    def state_body(first, dst):
        def body(c, carry):
            for r, act in enumerate(conv_block(c)):
                act = act.astype(dst.dtype)
                for k in range(per):
                    dst[(c - first) * per + k, r * CONV_ROWS:(r + 1) * CONV_ROWS, :] = (
                        act[:, k * STATE_DIM:(k + 1) * STATE_DIM])
            return carry
        return body

    lax.fori_loop(n_xs, n_xs + n_b, state_body(n_xs, bm_ref), 0)
    lax.fori_loop(n_xs + n_b, n_blocks, state_body(n_xs + n_b, cm_ref), 0)


def _conv_silu(proj, conv_w, conv_b, bsz, seq, first_block, n_xs):
    t = proj.shape[1]
    n_blocks = conv_w.shape[0]
    tr = _tile(seq, 512)
    nr = seq // tr
    hb = tr // HALO_ROWS
    last_halo = t // HALO_ROWS - 1
    n_bc = (n_blocks - n_xs) // 2 * (COL_BLOCK // STATE_DIM)
    cb0 = first_block // n_blocks
    assert cb0 * n_blocks == first_block
    assert tr % CONV_ROWS == 0
    est = (2 * n_blocks * (tr + 2 * HALO_ROWS) * COL_BLOCK * 2 + 2 * (n_xs * tr * COL_BLOCK + 2 * n_bc * tr * STATE_DIM) * 2
           + (tr + 2 * CONV_PAD) * COL_BLOCK * 2 + 4 * n_blocks * 8 * COL_BLOCK * 4 + 8 * 1024 * 1024)
    shift = _shift_matrix()
    kern = functools.partial(_conv_kernel, tr=tr, n_xs=n_xs)
    return pl.pallas_call(
        kern,
        grid=(bsz, nr),
        in_specs=[
            pl.BlockSpec((n_blocks, tr, COL_BLOCK), lambda b, i: (cb0, b * nr + i, 0)),
            pl.BlockSpec((n_blocks, HALO_ROWS, COL_BLOCK),
                         lambda b, i: (cb0, jnp.maximum((b * nr + i) * hb - 1, 0), 0)),
            pl.BlockSpec((n_blocks, HALO_ROWS, COL_BLOCK),
                         lambda b, i: (cb0, jnp.minimum((b * nr + i + 1) * hb, last_halo), 0)),
            pl.BlockSpec((n_blocks, CONV_WIDTH, COL_BLOCK), lambda b, i: (0, 0, 0)),
            pl.BlockSpec((n_blocks, 1, COL_BLOCK), lambda b, i: (0, 0, 0)),
            pl.BlockSpec(shift.shape, lambda b, i: (0, 0)),
        ],
        out_specs=[
            pl.BlockSpec((n_xs, tr, COL_BLOCK), lambda b, i: (0, b * nr + i, 0)),
            pl.BlockSpec((n_bc, tr, STATE_DIM), lambda b, i: (0, b * nr + i, 0)),
            pl.BlockSpec((n_bc, tr, STATE_DIM), lambda b, i: (0, b * nr + i, 0)),
        ],
        out_shape=[
            jax.ShapeDtypeStruct((n_xs, t, COL_BLOCK), BF16),
            jax.ShapeDtypeStruct((n_bc, t, STATE_DIM), BF16),
            jax.ShapeDtypeStruct((n_bc, t, STATE_DIM), BF16),
        ],
        scratch_shapes=[pltpu.VMEM((tr + 2 * CONV_PAD, COL_BLOCK), BF16)],
        compiler_params=_params(("parallel", "parallel"), est),
        name="conv_silu",
    )(proj, proj, proj, conv_w, conv_b, shift)


def _scan_rows(v, reverse):
    n = v.shape[0]
    row = lax.broadcasted_iota(jnp.int32, v.shape, 0)
    k = 1
    while k < n:
        if reverse:
            shifted = pltpu.roll(v, n - k, axis=0)
            keep = row < n - k
        else:
            shifted = pltpu.roll(v, k, axis=0)
            keep = row >= k
        v = v + jnp.where(keep, shifted, 0.0)
        k *= 2
    return v


def _ssd_chunk_prologue(dt_raw, dtb_ref, a_ref, a_sc, at_sc, wdt_sc, reverse):
    dt = _softplus(dt_raw + dtb_ref[...])
    acc = _scan_rows(dt * a_ref[...], reverse)
    a_sc[...] = acc * LOG2_E
    acc_t = acc.T
    dt_t = dt.T
    at_sc[...] = acc_t * LOG2_E - jnp.log2(dt_t)
    n = acc.shape[0]
    total = acc_t[:, 0:1] if reverse else acc_t[:, n - 1:n]
    wdt_sc[...] = jnp.exp(total - acc_t) * dt_t


def _ssd_group(g, rows, xs_ref, bm_ref, cm_ref, st_ref, a_sc, at_sc, wdt_sc, head0, reverse):
    l = CHUNK
    xs = xs_ref[g, rows, :]
    bm = bm_ref[g, rows, :]
    cm = cm_ref[g, rows, :]
    cb = lax.dot_general(cm, bm, (((1,), (1,)), ((), ())), preferred_element_type=F32).astype(BF16)
    bm_t = bm.astype(F32).T.astype(BF16)

    heads_per_group = COL_BLOCK // HEAD_DIM_SSM
    first = head0 + g * heads_per_group
    a_all = a_sc[...]
    row_i = lax.broadcasted_iota(jnp.int32, (l, l), 0)
    col_i = lax.broadcasted_iota(jnp.int32, (l, l), 1)
    causal = (row_i <= col_i) if reverse else (row_i >= col_i)
    lane = lax.broadcasted_iota(jnp.int32, (l, 2 * HEAD_DIM_SSM), 1)
    low_half = lane < HEAD_DIM_SSM
    keep_low = jnp.where(low_half, 1.0, 0.0).astype(BF16)
    keep_high = jnp.where(low_half, 0.0, 1.0).astype(BF16)
    edge = 0 if reverse else l - 1

    y_slabs = []
    for pair in range(heads_per_group // 2):
        sl = slice(pair * 2 * HEAD_DIM_SSM, (pair + 1) * 2 * HEAD_DIM_SSM)
        st_pair = st_ref[g, :, sl]
        y_off = jnp.dot(cm, st_pair.astype(BF16), preferred_element_type=F32)
        xs_pair = xs[:, sl]
        rhs = jnp.concatenate([xs_pair * keep_low, xs_pair * keep_high], axis=0)
        m_parts, bw_parts, a_col_parts, a_tot_parts = [], [], [], []
        for sub in range(2):
            h = first + pair * 2 + sub
            a_col = a_all[:, h:h + 1]
            a_row = at_sc[h:h + 1, :]
            wd_row = wdt_sc[h:h + 1, :]
            seg = jnp.where(causal, a_col - a_row, -jnp.inf)
            m_parts.append(cb * jnp.exp2(seg).astype(BF16))
            bw_parts.append(bm_t * wd_row.astype(BF16))
            a_col_parts.append(a_col)
            a_tot_parts.append(a_all[edge:edge + 1, h:h + 1])
        m_pair = jnp.concatenate(m_parts, axis=1)
        bw_pair = jnp.concatenate(bw_parts, axis=1)
        a_col_pair = jnp.where(low_half, a_col_parts[0], a_col_parts[1])
        a_tot_pair = jnp.where(low_half[0:1], a_tot_parts[0], a_tot_parts[1])
        y_slabs.append(jnp.dot(m_pair, rhs, preferred_element_type=F32) + jnp.exp2(a_col_pair) * y_off)
        st_ref[g, :, sl] = jnp.exp2(a_tot_pair) * st_pair + jnp.dot(bw_pair, rhs, preferred_element_type=F32)
    return jnp.concatenate(y_slabs, axis=1), xs


def _chunk_rows(index):
    return pl.ds(pl.multiple_of(index * CHUNK, CHUNK), CHUNK)


def _ssd_fwd_kernel(xs_ref, bm_ref, cm_ref, dt_ref, dtb_ref, a_ref, y_ref, st_ref, a_sc, at_sc, wdt_sc):
    @pl.when(pl.program_id(1) == 0)
    def _():
        st_ref[...] = jnp.zeros_like(st_ref)

    def chunk(c, carry):
        rows = _chunk_rows(c)
        _ssd_chunk_prologue(dt_ref[rows, :], dtb_ref, a_ref, a_sc, at_sc, wdt_sc, reverse=False)
        for g in range(SSM_GROUPS):
            y, _ = _ssd_group(g, rows, xs_ref, bm_ref, cm_ref, st_ref, a_sc, at_sc, wdt_sc, head0=0, reverse=False)
            y_ref[g, rows, :] = y.astype(y_ref.dtype)
        return carry

    lax.fori_loop(0, dt_ref.shape[0] // CHUNK, chunk, 0)


def _ssd_bwd_kernel(xs_ref, bm_ref, cm_ref, dt_ref, dtb_ref, a_ref, yf_ref, z_ref, dskip_ref, nw_ref, y_ref,
                    st_ref, a_sc, at_sc, wdt_sc, *, n_heads):
    @pl.when(pl.program_id(1) == 0)
    def _():
        st_ref[...] = jnp.zeros_like(st_ref)

    n_chunks = dt_ref.shape[0] // CHUNK

    def chunk(it, carry):
        rows = _chunk_rows(n_chunks - 1 - it)
        _ssd_chunk_prologue(dt_ref[rows, :], dtb_ref, a_ref, a_sc, at_sc, wdt_sc, reverse=True)
        for g in range(SSM_GROUPS):
            y, xs = _ssd_group(g, rows, xs_ref, bm_ref, cm_ref, st_ref, a_sc, at_sc, wdt_sc, head0=n_heads,
                               reverse=True)
            y = y + yf_ref[g, rows, :].astype(F32) + xs.astype(F32) * dskip_ref[g]
            y = y * _silu(z_ref[g, rows, :].astype(F32))
            y_ref[g, rows, :] = _rmsnorm(y, nw_ref[g]).astype(y_ref.dtype)
        return carry

    lax.fori_loop(0, n_chunks, chunk, 0)


def _ssd_scratch(two_h):
    return [
        pltpu.VMEM((SSM_GROUPS, STATE_DIM, COL_BLOCK), F32),
        pltpu.VMEM((CHUNK, two_h), F32),
        pltpu.VMEM((two_h, CHUNK), F32),
        pltpu.VMEM((two_h, CHUNK), F32),
    ]


def _ssd_est(two_h, n_slabs, rows):
    return (2 * n_slabs * SSM_GROUPS * rows * COL_BLOCK * 2 + 4 * SSM_GROUPS * rows * STATE_DIM * 2
            + SSM_GROUPS * STATE_DIM * COL_BLOCK * 4 + 2 * rows * two_h * 4 + 6 * CHUNK * two_h * 4 + 8 * 1024 * 1024)


def _ssd_forward(xs, bm, cm, dt_raw, dt_bias, a_neg, bsz, seq):
    g, t, _ = xs.shape
    rows = _tile(seq, SSD_STEP_CHUNKS * CHUNK)
    nc = seq // rows
    two_h = dt_raw.shape[1]
    slab = pl.BlockSpec((g, rows, COL_BLOCK), lambda b, c: (0, b * nc + c, 0))
    bc = pl.BlockSpec((g, rows, STATE_DIM), lambda b, c: (0, b * nc + c, 0))
    row = pl.BlockSpec((1, two_h), lambda b, c: (0, 0))
    return pl.pallas_call(
        _ssd_fwd_kernel,
        grid=(bsz, nc),
        in_specs=[slab, bc, bc, pl.BlockSpec((rows, two_h), lambda b, c: (b * nc + c, 0)), row, row],
        out_specs=slab,
        out_shape=jax.ShapeDtypeStruct((g, t, COL_BLOCK), BF16),
        scratch_shapes=_ssd_scratch(two_h),
        compiler_params=_params(("parallel", "arbitrary"), _ssd_est(two_h, 2, rows)),
        name="ssd_fwd",
    )(xs, bm, cm, dt_raw, dt_bias, a_neg)


def _ssd_backward(xs, bm, cm, dt_raw, dt_bias, a_neg, y_fwd, proj, d_skip, norm_w, bsz, seq, z_block):
    g, t, _ = xs.shape
    rows = _tile(seq, SSD_STEP_CHUNKS * CHUNK)
    nc = seq // rows
    two_h = dt_raw.shape[1]
    rev = lambda b, c: (0, b * nc + nc - 1 - c, 0)
    slab = pl.BlockSpec((g, rows, COL_BLOCK), rev)
    bc = pl.BlockSpec((g, rows, STATE_DIM), rev)
    row = pl.BlockSpec((1, two_h), lambda b, c: (0, 0))
    par = pl.BlockSpec((g, 1, COL_BLOCK), lambda b, c: (0, 0, 0))
    zb = z_block // g
    assert zb * g == z_block
    kern = functools.partial(_ssd_bwd_kernel, n_heads=two_h // 2)
    return pl.pallas_call(
        kern,
        grid=(bsz, nc),
        in_specs=[slab, bc, bc, pl.BlockSpec((rows, two_h), lambda b, c: (b * nc + nc - 1 - c, 0)), row, row,
                  slab, pl.BlockSpec((g, rows, COL_BLOCK), lambda b, c: (zb, b * nc + nc - 1 - c, 0)), par, par],
        out_specs=slab,
        out_shape=jax.ShapeDtypeStruct((g, t, COL_BLOCK), BF16),
        scratch_shapes=_ssd_scratch(two_h),
        compiler_params=_params(("parallel", "arbitrary"), _ssd_est(two_h, 4, rows)),
        name="ssd_bwd",
    )(xs, bm, cm, dt_raw, dt_bias, a_neg, y_fwd, proj, d_skip, norm_w)


def _t5_buckets(rel):
    half = REL_BUCKETS // 2
    ret = (rel > 0).astype(np.int32) * half
    n = np.abs(rel)
    max_exact = half // 2
    large = max_exact + (np.log(np.maximum(n, 1) / max_exact) / np.log(REL_MAX_DIST / max_exact)
                         * (half - max_exact)).astype(np.int32)
    large = np.minimum(large, half - 1)
    return ret + np.where(n < max_exact, n, large).astype(np.int32)


def _bias_kernel(rel_bias_ref, bucket_ref, o_ref):
    h = pl.program_id(0)
    bucket = bucket_ref[...]
    acc = jnp.zeros(bucket.shape, F32)
    for b in range(REL_BUCKETS):
        acc = jnp.where(bucket == b, rel_bias_ref[b, h], acc)
    o_ref[...] = jnp.where(bucket < 0, MASK_VALUE, acc * LOG2_E)


def _attention_bias(rel_bias):
    blk = ATTN_BLOCK
    rel = np.arange(3 * blk)[None, :] - blk - np.arange(blk)[:, None]
    bucket = np.where(np.abs(rel) <= WINDOW, _t5_buckets(rel), -1).astype(np.int32)
    heads = rel_bias.shape[1]
    return pl.pallas_call(
        _bias_kernel,
        grid=(heads,),
        in_specs=[pl.BlockSpec(memory_space=pltpu.SMEM), pl.BlockSpec((blk, 3 * blk), lambda h: (0, 0))],
        out_specs=pl.BlockSpec((None, blk, 3 * blk), lambda h: (h, 0, 0)),
        out_shape=jax.ShapeDtypeStruct((heads, blk, 3 * blk), F32),
        compiler_params=_params(("arbitrary",), 4 * 1024 * 1024),
        name="attn_bias",
    )(rel_bias, jnp.asarray(bucket))


def _attn_kernel(sink_ref, q_ref, k_ref, kp_ref, kn_ref, v_ref, vp_ref, vn_ref, bias_ref, o_ref, kw_ref, vw_ref,
                 *, tq, seq, scale):
    blk = ATTN_BLOCK
    d = ATTN_HEAD_DIM
    kw_ref[0:blk, :] = kp_ref[...]
    kw_ref[blk:blk + tq, :] = k_ref[...]
    kw_ref[blk + tq:, :] = kn_ref[...]
    vw_ref[0:blk, :] = vp_ref[...]
    vw_ref[blk:blk + tq, :] = v_ref[...]
    vw_ref[blk + tq:, :] = vn_ref[...]
    step0 = pl.program_id(1) * tq
    key_off = lax.broadcasted_iota(jnp.int32, (1, 3 * blk), 1) - blk
    n_sub = tq // blk
    for g in range(KV_HEADS):
        for i in range(n_sub):
            kwin = kw_ref[i * blk:(i + 3) * blk, g * d:(g + 1) * d]
            vwin = vw_ref[i * blk:(i + 3) * blk, g * d:(g + 1) * d]
            q_rows = q_ref[g, i * blk:(i + 1) * blk, :]
            q4 = jnp.concatenate([q_rows[:, r * d:(r + 1) * d] for r in range(Q_PER_KV)], axis=0)
            logits = lax.dot_general(q4, kwin, (((1,), (1,)), ((), ())), preferred_element_type=F32)
            logits = logits * (scale * LOG2_E) + bias_ref[g]
            if i == 0 or i == n_sub - 1:
                key_pos = step0 + i * blk + key_off
                logits = jnp.where((key_pos >= 0) & (key_pos < seq), logits, MASK_VALUE)
            probs, inv = [], []
            for r in range(Q_PER_KV):
                lg = logits[r * blk:(r + 1) * blk]
                sink = sink_ref[g * Q_PER_KV + r] * LOG2_E
                m = jnp.maximum(jnp.max(lg, axis=-1, keepdims=True), sink)
                p = jnp.exp2(lg - m)
                inv.append(1.0 / (jnp.sum(p, axis=-1, keepdims=True) + jnp.exp2(sink - m)))
                probs.append(p.astype(BF16))
            out = jnp.dot(jnp.concatenate(probs, axis=0), vwin, preferred_element_type=F32)
            for r in range(Q_PER_KV):
                o_ref[g, i * blk:(i + 1) * blk, r * d:(r + 1) * d] = (
                    out[r * blk:(r + 1) * blk] * inv[r]).astype(o_ref.dtype)


def _attention(proj, bias, sink, bsz, seq, q_block, k_block, v_block):
    t = proj.shape[1]
    blk = ATTN_BLOCK
    tq = _tile(seq, 512)
    nq = seq // tq
    per = tq // blk
    last = t // blk - 1
    width = Q_PER_KV * ATTN_HEAD_DIM
    assert width == COL_BLOCK and KV_HEADS * ATTN_HEAD_DIM == COL_BLOCK
    qb = q_block // KV_HEADS
    assert qb * KV_HEADS == q_block
    main = lambda cb: pl.BlockSpec((None, tq, COL_BLOCK), lambda b, i: (cb, b * nq + i, 0))
    prev = lambda cb: pl.BlockSpec((None, blk, COL_BLOCK), lambda b, i: (cb, jnp.maximum((b * nq + i) * per - 1, 0), 0))
    nxt = lambda cb: pl.BlockSpec((None, blk, COL_BLOCK), lambda b, i: (cb, jnp.minimum((b * nq + i + 1) * per, last), 0))
    kern = functools.partial(_attn_kernel, tq=tq, seq=seq, scale=ATTN_HEAD_DIM ** -0.5)
    est = (2 * KV_HEADS * tq * COL_BLOCK * 2 * 2 + 4 * (tq + 2 * blk) * COL_BLOCK * 2 + 2 * (tq + 2 * blk) * COL_BLOCK * 2
           + 2 * KV_HEADS * Q_PER_KV * blk * 3 * blk * 4 + 8 * 1024 * 1024)
    window = pltpu.VMEM((tq + 2 * blk, COL_BLOCK), BF16)
    return pl.pallas_call(
        kern,
        grid=(bsz, nq),
        in_specs=[
            pl.BlockSpec(memory_space=pltpu.SMEM),
            pl.BlockSpec((KV_HEADS, tq, COL_BLOCK), lambda b, i: (qb, b * nq + i, 0)),
            main(k_block), prev(k_block), nxt(k_block),
            main(v_block), prev(v_block), nxt(v_block),
            pl.BlockSpec((KV_HEADS, Q_PER_KV * blk, 3 * blk), lambda b, i: (0, 0, 0)),
        ],
        out_specs=pl.BlockSpec((KV_HEADS, tq, COL_BLOCK), lambda b, i: (0, b * nq + i, 0)),
        out_shape=jax.ShapeDtypeStruct((KV_HEADS, t, COL_BLOCK), BF16),
        scratch_shapes=[window, window],
        compiler_params=_params(("parallel", "parallel"), est),
        name="attention",
    )(sink, proj, proj, proj, proj, proj, proj, proj, bias)


def _merge_kernel(ys_ref, at_ref, ga_ref, gb_ref, ws_ref, wa_ref, o_ref):
    def branch(lhs_ref, w_ref):
        acc = None
        for c in range(lhs_ref.shape[0]):
            part = jnp.dot(lhs_ref[c], w_ref[c * COL_BLOCK:(c + 1) * COL_BLOCK, :], preferred_element_type=F32)
            acc = part if acc is None else acc + part
        return acc

    ya = branch(ys_ref, ws_ref)
    yb = branch(at_ref, wa_ref)
    merged = _sigmoid(ga_ref[...].astype(F32)) * ya + _sigmoid(gb_ref[...].astype(F32)) * yb
    o_ref[...] = merged.astype(o_ref.dtype)


def _merge(y_ssm, attn, proj, w_ssm, w_attn, gate_block):
    ks, t, _ = y_ssm.shape
    ka = attn.shape[0]
    dm = w_ssm.shape[1]
    nb = dm // COL_BLOCK
    tm = _tile(t, 1024)
    est = 2 * (ks + ka) * tm * COL_BLOCK * 2 + 2 * (ks + ka) * COL_BLOCK * COL_BLOCK * 2 + 6 * tm * COL_BLOCK * 2 + 3 * tm * COL_BLOCK * 4
    return pl.pallas_call(
        _merge_kernel,
        grid=(t // tm, nb),
        in_specs=[
            pl.BlockSpec((ks, tm, COL_BLOCK), lambda i, j: (0, i, 0)),
            pl.BlockSpec((ka, tm, COL_BLOCK), lambda i, j: (0, i, 0)),
            pl.BlockSpec((None, tm, COL_BLOCK), lambda i, j: (gate_block + _serpentine(i, j, nb), i, 0)),
            pl.BlockSpec((None, tm, COL_BLOCK), lambda i, j: (gate_block + nb + _serpentine(i, j, nb), i, 0)),
            pl.BlockSpec((ks * COL_BLOCK, COL_BLOCK), lambda i, j: (0, _serpentine(i, j, nb))),
            pl.BlockSpec((ka * COL_BLOCK, COL_BLOCK), lambda i, j: (0, _serpentine(i, j, nb))),
        ],
        out_specs=pl.BlockSpec((None, tm, COL_BLOCK), lambda i, j: (_serpentine(i, j, nb), i, 0)),
        out_shape=jax.ShapeDtypeStruct((nb, t, COL_BLOCK), BF16),
        compiler_params=_params(("parallel", "arbitrary"), est),
        name="merge",
    )(y_ssm, attn, proj, proj, w_ssm, w_attn)


def _outproj_kernel(m_ref, w_ref, x_ref, o_ref):
    acc = x_ref[...]
    for c in range(m_ref.shape[0]):
        acc = acc + jnp.dot(m_ref[c], w_ref[c * COL_BLOCK:(c + 1) * COL_BLOCK, :], preferred_element_type=F32)
    o_ref[...] = acc


def _out_proj(merged, w_out, x):
    kb, t, _ = merged.shape
    d = w_out.shape[1]
    tm = _tile(t, 512)
    tn = d
    est = 2 * kb * tm * COL_BLOCK * 2 + 2 * kb * COL_BLOCK * tn * 2 + 4 * tm * tn * 4 + tm * tn * 4
    return pl.pallas_call(
        _outproj_kernel,
        grid=(t // tm, d // tn),
        in_specs=[
            pl.BlockSpec((kb, tm, COL_BLOCK), lambda i, j: (0, i, 0)),
            pl.BlockSpec((kb * COL_BLOCK, tn), lambda i, j: (0, j)),
            pl.BlockSpec((tm, tn), lambda i, j: (i, j)),
        ],
        out_specs=pl.BlockSpec((tm, tn), lambda i, j: (i, j)),
        out_shape=jax.ShapeDtypeStruct((t, d), F32),
        compiler_params=_params(("parallel", "arbitrary"), est),
        name="out_proj",
    )(merged, w_out, x)


def _ffn_kernel(h_ref, nw_ref, wg_ref, wu_ref, wo_ref, fw_ref, o_ref, hn_ref, acc_ref, *, final_norm):
    f = pl.program_id(1)

    @pl.when(f == 0)
    def _():
        hn_ref[...] = _rmsnorm(h_ref[...], nw_ref[...]).astype(BF16)
        acc_ref[...] = jnp.zeros_like(acc_ref)

    hn = hn_ref[...]
    tf = wg_ref.shape[1]
    part = tf // FFN_SPLIT
    acc = acc_ref[...]
    for s in range(FFN_SPLIT):
        cols = slice(s * part, (s + 1) * part)
        gate = jnp.dot(hn, wg_ref[:, cols], preferred_element_type=F32)
        up = jnp.dot(hn, wu_ref[:, cols], preferred_element_type=F32)
        act = (_silu(gate) * up).astype(BF16)
        acc = acc + jnp.dot(act, wo_ref[cols, :], preferred_element_type=F32)
    acc_ref[...] = acc

    @pl.when(f == pl.num_programs(1) - 1)
    def _():
        out = h_ref[...] + acc_ref[...]
        o_ref[...] = _rmsnorm(out, fw_ref[...]) if final_norm else out


def _ffn(h, norm_w, w_in, w_out, final_w, final_norm):
    t, d = h.shape
    ff = w_out.shape[0]
    tm = _tile(t, 512)
    tf = _tile(ff, 512)
    nf = ff // tf
    est = 2 * tm * d * 4 * 2 + tm * d * 2 + tm * d * 4 + 2 * 3 * d * tf * 2 + 4 * tm * tf * 4
    return pl.pallas_call(
        functools.partial(_ffn_kernel, final_norm=final_norm),
        grid=(t // tm, nf),
        in_specs=[
            pl.BlockSpec((tm, d), lambda i, f: (i, 0)),
            pl.BlockSpec((1, d), lambda i, f: (0, 0)),
            pl.BlockSpec((d, tf), lambda i, f: (0, _serpentine(i, f, nf))),
            pl.BlockSpec((d, tf), lambda i, f: (0, nf + _serpentine(i, f, nf))),
            pl.BlockSpec((tf, d), lambda i, f: (_serpentine(i, f, nf), 0)),
            pl.BlockSpec((1, d), lambda i, f: (0, 0)),
        ],
        out_specs=pl.BlockSpec((tm, d), lambda i, f: (i, 0)),
        out_shape=jax.ShapeDtypeStruct((t, d), F32),
        scratch_shapes=[pltpu.VMEM((tm, d), BF16), pltpu.VMEM((tm, d), F32)],
        compiler_params=_params(("parallel", "arbitrary"), est),
        name="ffn",
    )(h, norm_w.reshape(1, d), w_in, w_in, w_out, final_w.reshape(1, d))


def _layer_weights(w_in, conv_w, conv_b, dt_bias, a_log, d_skip, ssm_norm_w, d_model):
    d_inner = ssm_norm_w.shape[0]
    n_heads = d_skip.shape[0]
    gn = SSM_GROUPS * STATE_DIM
    conv_dim = d_inner + 2 * gn
    q_dim = d_model
    kv_dim = KV_HEADS * ATTN_HEAD_DIM
    cuts = np.cumsum([d_inner, conv_dim, 2 * n_heads, q_dim, kv_dim, kv_dim, 2 * d_model])
    z_w, xbc_w, dt_w, q_w, k_w, v_w, gate_w = jnp.split(w_in, cuts[:-1], axis=1)
    w_main = jnp.concatenate([z_w, q_w, xbc_w, gate_w, k_w, v_w], axis=1).astype(BF16)
    blocks = {}
    off = 0
    for name, width in (("z", d_inner), ("q", q_dim), ("xbc", conv_dim), ("gate", 2 * d_model), ("k", kv_dim), ("v", kv_dim)):
        blocks[name] = off // COL_BLOCK
        off += width
    n_conv = conv_dim // COL_BLOCK
    conv_w3 = conv_w.reshape(CONV_WIDTH, n_conv, COL_BLOCK).transpose(1, 0, 2)
    conv_b3 = conv_b.reshape(n_conv, 1, COL_BLOCK)
    dtb = dt_bias.reshape(1, 2 * n_heads)
    a_neg = (-jnp.exp(a_log.astype(F32))).reshape(1, 2 * n_heads)
    dskip = jnp.repeat(d_skip.astype(F32), HEAD_DIM_SSM).reshape(SSM_GROUPS, 1, COL_BLOCK)
    nw = ssm_norm_w.astype(F32).reshape(SSM_GROUPS, 1, COL_BLOCK)
    return w_main, dt_w.astype(BF16), blocks, conv_w3, conv_b3, dtb, a_neg, dskip, nw


def _trunk(x, mix_norm_w, w_in, conv_w, conv_b, dt_bias, a_log, d_skip, ssm_norm_w, w_ssm_branch, rel_bias,
           attn_sink, w_attn_branch, w_out, ffn_norm_w, w_ffn_in, w_ffn_out, final_norm_w):
    bsz, seq, d_model = x.shape
    depth = w_in.shape[0]
    h = x.reshape(bsz * seq, d_model)
    bias = _attention_bias(rel_bias).reshape(KV_HEADS, Q_PER_KV * ATTN_BLOCK, 3 * ATTN_BLOCK)
    for l in range(depth):
        w_main, w_dt, blocks, conv_w3, conv_b3, dtb, a_neg, dskip, nw = _layer_weights(
            w_in[l], conv_w[l], conv_b[l], dt_bias[l], a_log[l], d_skip[l], ssm_norm_w[l], d_model)
        proj, dt_raw = _in_proj(h, mix_norm_w[l], w_main, w_dt)
        n_xs = ssm_norm_w.shape[1] // COL_BLOCK
        xs, bm, cm = _conv_silu(proj, conv_w3, conv_b3, bsz, seq, blocks["xbc"], n_xs)
        y_fwd = _ssd_forward(xs, bm, cm, dt_raw, dtb, a_neg, bsz, seq)
        y_ssm = _ssd_backward(xs, bm, cm, dt_raw, dtb, a_neg, y_fwd, proj, dskip, nw, bsz, seq, blocks["z"])
        attn = _attention(proj, bias, attn_sink[l], bsz, seq, blocks["q"], blocks["k"], blocks["v"])
        merged = _merge(y_ssm, attn, proj, w_ssm_branch[l].astype(BF16), w_attn_branch[l].astype(BF16), blocks["gate"])
        h = _out_proj(merged, w_out[l].astype(BF16), h)
        h = _ffn(h, ffn_norm_w[l], w_ffn_in[l].astype(BF16), w_ffn_out[l].astype(BF16), final_norm_w,
                 final_norm=l == depth - 1)
    return h.reshape(bsz, seq, d_model)


def kernel(x_prompt, x_sample, mix_norm_w, w_in, conv_w, conv_b, dt_bias, a_log, d_skip, ssm_norm_w, w_ssm_branch,
           rel_bias, attn_sink, w_attn_branch, w_out, ffn_norm_w, w_ffn_in, w_ffn_out, final_norm_w):
    args = (mix_norm_w, w_in, conv_w, conv_b, dt_bias, a_log, d_skip, ssm_norm_w, w_ssm_branch, rel_bias, attn_sink,
            w_attn_branch, w_out, ffn_norm_w, w_ffn_in, w_ffn_out, final_norm_w)
    return (_trunk(x_prompt, *args), _trunk(x_sample, *args))
```

```python
import functools
import math

import numpy as np
import jax
import jax.numpy as jnp
from jax import lax
from jax.experimental import pallas as pl
from jax.experimental.pallas import tpu as pltpu

F32 = jnp.float32
BF16 = jnp.bfloat16

EPS = 1e-6
HEAD_DIM_SSM = 64
STATE_DIM = 128
SSM_GROUPS = 8
CHUNK = 128
FFN_SPLIT = 2
SSD_STEP_CHUNKS = 4
FWD_WIDE_STATE = True
BWD_WIDE_STATE = False
CONV_WIDTH = 5
ATTN_HEAD_DIM = 128
KV_HEADS = 4
Q_PER_KV = 4
WINDOW = 128
ATTN_BLOCK = 128
REL_BUCKETS = 32
REL_MAX_DIST = 128
COL_BLOCK = 512
SUBLANES = 8
HALO_ROWS = 16
CONV_ROWS = 128
CONV_PAD = 64
LOG2_E = math.log2(math.e)
MASK_VALUE = -1e30
VMEM_CAP = 60000 * 1024


def _vmem_limit(estimate_bytes):
    return int(min(VMEM_CAP, max(estimate_bytes * 5 // 4, 16 * 1024 * 1024)))


def _params(semantics, vmem_estimate):
    return pltpu.CompilerParams(dimension_semantics=semantics,
                                vmem_limit_bytes=_vmem_limit(vmem_estimate))


def _sigmoid(x):
    return 1.0 / (1.0 + jnp.exp2(x * -LOG2_E))


def _silu(x):
    return x * _sigmoid(x)


def _softplus(x):
    return jnp.maximum(x, 0.0) + jnp.log1p(jnp.exp(-jnp.abs(x)))


def _rmsnorm(x, w):
    ms = jnp.mean(x * x, axis=-1, keepdims=True)
    return x * lax.rsqrt(ms + EPS) * w


def _tile(n, want):
    t = min(n, want)
    while n % t:
        t //= 2
    return t


def _serpentine(i, j, n):
    return jnp.where(i % 2 == 0, j, n - 1 - j)


def _inproj_kernel(x_ref, nw_ref, w_ref, wdt_ref, o_ref, dt_ref, xn_ref):
    @pl.when(pl.program_id(1) == 0)
    def _():
        xn = _rmsnorm(x_ref[...], nw_ref[...]).astype(BF16)
        xn_ref[...] = xn
        dt_ref[...] = jnp.dot(xn, wdt_ref[...], preferred_element_type=F32)

    out = jnp.dot(xn_ref[...], w_ref[...], preferred_element_type=F32).astype(o_ref.dtype)
    for c in range(o_ref.shape[0]):
        o_ref[c] = out[:, c * COL_BLOCK:(c + 1) * COL_BLOCK]


def _in_proj(x, norm_w, w_main, w_dt):
    t, d = x.shape
    n = w_main.shape[1]
    nb = n // COL_BLOCK
    tm = _tile(t, 1024)
    per = 2 if nb % 2 == 0 else 1
    tn = per * COL_BLOCK
    ndt = w_dt.shape[1]
    nj = nb // per
    est = 2 * tm * d * 4 + tm * d * 2 + 2 * d * tn * 2 + 2 * tm * tn * 2 + tm * tn * 4 + 2 * d * ndt * 2 + 2 * tm * ndt * 4
    return pl.pallas_call(
        _inproj_kernel,
        grid=(t // tm, nj),
        in_specs=[
            pl.BlockSpec((tm, d), lambda i, j: (i, 0)),
            pl.BlockSpec((1, d), lambda i, j: (0, 0)),
            pl.BlockSpec((d, tn), lambda i, j: (0, _serpentine(i, j, nj))),
            pl.BlockSpec((d, ndt), lambda i, j: (0, 0)),
        ],
        out_specs=[
            pl.BlockSpec((per, tm, COL_BLOCK), lambda i, j: (_serpentine(i, j, nj), i, 0)),
            pl.BlockSpec((tm, ndt), lambda i, j: (i, 0)),
        ],
        out_shape=[
            jax.ShapeDtypeStruct((nb, t, COL_BLOCK), BF16),
            jax.ShapeDtypeStruct((t, ndt), F32),
        ],
        scratch_shapes=[pltpu.VMEM((tm, d), BF16)],
        compiler_params=_params(("parallel", "arbitrary"), est),
        name="in_proj",
    )(x, norm_w.reshape(1, d), w_main, w_dt)


def _shift_matrix():
    half = CONV_WIDTH // 2
    n_shift = CONV_WIDTH - 1
    rows = np.arange(n_shift * CONV_ROWS)[:, None]
    cols = np.arange(CONV_ROWS + 2 * CONV_PAD)[None, :]
    q, rem = rows // (n_shift * SUBLANES), rows % (n_shift * SUBLANES)
    blk, t = rem // SUBLANES, q * SUBLANES + rem % SUBLANES
    off = blk - half + (blk >= half)
    return jnp.asarray(cols == t + CONV_PAD + off, BF16)


def _conv_kernel(main_ref, prev_ref, next_ref, w_ref, b_ref, shift_ref, xs_ref, bm_ref, cm_ref, win_ref, *, tr, n_xs):
    i = pl.program_id(1)
    last = pl.num_programs(1) - 1
    half = CONV_WIDTH // 2
    n_blocks = main_ref.shape[0]
    shift = shift_ref[...]
    win_ref[0:CONV_PAD - HALO_ROWS, :] = jnp.zeros((CONV_PAD - HALO_ROWS, COL_BLOCK), BF16)
    win_ref[CONV_PAD + tr + HALO_ROWS:, :] = jnp.zeros((CONV_PAD - HALO_ROWS, COL_BLOCK), BF16)

    def conv_block(c):
        halo_zero = jnp.zeros((HALO_ROWS, COL_BLOCK), BF16)
        win_ref[CONV_PAD - HALO_ROWS:CONV_PAD, :] = jnp.where(i > 0, prev_ref[c], halo_zero)
        win_ref[CONV_PAD:CONV_PAD + tr, :] = main_ref[c]
        win_ref[CONV_PAD + tr:CONV_PAD + tr + HALO_ROWS, :] = jnp.where(i < last, next_ref[c], halo_zero)
        w = w_ref[c]
        bias = b_ref[c]
        outs = []
        for r in range(tr // CONV_ROWS):
            xwin = win_ref[r * CONV_ROWS:(r + 1) * CONV_ROWS + 2 * CONV_PAD, :]
            taps = jnp.dot(shift, xwin, preferred_element_type=F32)
            centre = xwin[CONV_PAD:CONV_PAD + CONV_ROWS].astype(F32)
            groups = []
            for q in range(CONV_ROWS // SUBLANES):
                acc = bias + centre[q * SUBLANES:(q + 1) * SUBLANES] * w[half:half + 1]
                for b in range(CONV_WIDTH - 1):
                    j = b + (b >= half)
                    row0 = (q * (CONV_WIDTH - 1) + b) * SUBLANES
                    acc = acc + taps[row0:row0 + SUBLANES] * w[j:j + 1]
                groups.append(acc)
            outs.append(_silu(jnp.concatenate(groups, axis=0)))
        return outs

    def xs_body(c, carry):
        for r, act in enumerate(conv_block(c)):
            xs_ref[c, r * CONV_ROWS:(r + 1) * CONV_ROWS, :] = act.astype(xs_ref.dtype)
        return carry

    lax.fori_loop(0, n_xs, xs_body, 0)
    per = COL_BLOCK // STATE_DIM
    n_b = (n_blocks - n_xs) // 2

    def state_body(first, dst):
        def body(c, carry):
            for r, act in enumerate(conv_block(c)):
                act = act.astype(dst.dtype)
                for k in range(per):
                    dst[(c - first) * per + k, r * CONV_ROWS:(r + 1) * CONV_ROWS, :] = (
                        act[:, k * STATE_DIM:(k + 1) * STATE_DIM])
            return carry
        return body

    lax.fori_loop(n_xs, n_xs + n_b, state_body(n_xs, bm_ref), 0)
    lax.fori_loop(n_xs + n_b, n_blocks, state_body(n_xs + n_b, cm_ref), 0)


def _conv_silu(proj, conv_w, conv_b, bsz, seq, first_block, n_xs):
    t = proj.shape[1]
    n_blocks = conv_w.shape[0]
    tr = _tile(seq, 512)
    nr = seq // tr
    hb = tr // HALO_ROWS
    last_halo = t // HALO_ROWS - 1
    n_bc = (n_blocks - n_xs) // 2 * (COL_BLOCK // STATE_DIM)
    cb0 = first_block // n_blocks
    assert cb0 * n_blocks == first_block
    assert tr % CONV_ROWS == 0
    est = (2 * n_blocks * (tr + 2 * HALO_ROWS) * COL_BLOCK * 2 + 2 * (n_xs * tr * COL_BLOCK + 2 * n_bc * tr * STATE_DIM) * 2
           + (tr + 2 * CONV_PAD) * COL_BLOCK * 2 + 4 * n_blocks * 8 * COL_BLOCK * 4 + 8 * 1024 * 1024)
    shift = _shift_matrix()
    kern = functools.partial(_conv_kernel, tr=tr, n_xs=n_xs)
    return pl.pallas_call(
        kern,
        grid=(bsz, nr),
        in_specs=[
            pl.BlockSpec((n_blocks, tr, COL_BLOCK), lambda b, i: (cb0, b * nr + i, 0)),
            pl.BlockSpec((n_blocks, HALO_ROWS, COL_BLOCK),
                         lambda b, i: (cb0, jnp.maximum((b * nr + i) * hb - 1, 0), 0)),
            pl.BlockSpec((n_blocks, HALO_ROWS, COL_BLOCK),
                         lambda b, i: (cb0, jnp.minimum((b * nr + i + 1) * hb, last_halo), 0)),
            pl.BlockSpec((n_blocks, CONV_WIDTH, COL_BLOCK), lambda b, i: (0, 0, 0)),
            pl.BlockSpec((n_blocks, 1, COL_BLOCK), lambda b, i: (0, 0, 0)),
            pl.BlockSpec(shift.shape, lambda b, i: (0, 0)),
        ],
        out_specs=[
            pl.BlockSpec((n_xs, tr, COL_BLOCK), lambda b, i: (0, b * nr + i, 0)),
            pl.BlockSpec((n_bc, tr, STATE_DIM), lambda b, i: (0, b * nr + i, 0)),
            pl.BlockSpec((n_bc, tr, STATE_DIM), lambda b, i: (0, b * nr + i, 0)),
        ],
        out_shape=[
            jax.ShapeDtypeStruct((n_xs, t, COL_BLOCK), BF16),
            jax.ShapeDtypeStruct((n_bc, t, STATE_DIM), BF16),
            jax.ShapeDtypeStruct((n_bc, t, STATE_DIM), BF16),
        ],
        scratch_shapes=[pltpu.VMEM((tr + 2 * CONV_PAD, COL_BLOCK), BF16)],
        compiler_params=_params(("parallel", "parallel"), est),
        name="conv_silu",
    )(proj, proj, proj, conv_w, conv_b, shift)


def _scan_rows(v, reverse):
    n = v.shape[0]
    row = lax.broadcasted_iota(jnp.int32, v.shape, 0)
    k = 1
    while k < n:
        if reverse:
            shifted = pltpu.roll(v, n - k, axis=0)
            keep = row < n - k
        else:
            shifted = pltpu.roll(v, k, axis=0)
            keep = row >= k
        v = v + jnp.where(keep, shifted, 0.0)
        k *= 2
    return v


def _ssd_chunk_prologue(dt_raw, dtb_ref, a_ref, a_sc, at_sc, wdt_sc, reverse, wide_state):
    dt = _softplus(dt_raw + dtb_ref[...])
    acc = _scan_rows(dt * a_ref[...], reverse)
    a_sc[...] = acc * LOG2_E
    acc_t = acc.T
    dt_t = dt.T
    at_sc[...] = acc_t * LOG2_E - jnp.log2(dt_t)
    n = acc.shape[0]
    edge = 0 if reverse else n - 1
    if wide_state:
        wdt_sc[...] = jnp.exp(acc[edge:edge + 1, :] - acc) * dt
    else:
        wdt_sc[...] = jnp.exp(acc_t[:, edge:edge + 1] - acc_t) * dt_t


def _ssd_group(g, rows, xs_ref, bm_ref, cm_ref, st_ref, a_sc, at_sc, wdt_sc, head0, reverse, wide_state):
    l = CHUNK
    xs = xs_ref[g, rows, :]
    bm = bm_ref[g, rows, :]
    cm = cm_ref[g, rows, :]
    cb = lax.dot_general(cm, bm, (((1,), (1,)), ((), ())), preferred_element_type=F32).astype(BF16)
    bm_t = bm.astype(F32).T.astype(BF16)

    heads_per_group = COL_BLOCK // HEAD_DIM_SSM
    first = head0 + g * heads_per_group
    a_all = a_sc[...]
    row_i = lax.broadcasted_iota(jnp.int32, (l, l), 0)
    col_i = lax.broadcasted_iota(jnp.int32, (l, l), 1)
    causal = (row_i <= col_i) if reverse else (row_i >= col_i)
    lane = lax.broadcasted_iota(jnp.int32, (l, 2 * HEAD_DIM_SSM), 1)
    low_half = lane < HEAD_DIM_SSM
    keep_low = jnp.where(low_half, 1.0, 0.0).astype(BF16)
    keep_high = jnp.where(low_half, 0.0, 1.0).astype(BF16)
    edge = 0 if reverse else l - 1

    w_all = wdt_sc[...] if wide_state else None
    y_slabs, xw_slabs, tot_slabs = [], [], []
    for pair in range(heads_per_group // 2):
        sl = slice(pair * 2 * HEAD_DIM_SSM, (pair + 1) * 2 * HEAD_DIM_SSM)
        st_pair = st_ref[g, :, sl]
        y_off = jnp.dot(cm, st_pair.astype(BF16), preferred_element_type=F32)
        xs_pair = xs[:, sl]
        rhs = jnp.concatenate([xs_pair * keep_low, xs_pair * keep_high], axis=0)
        h0 = first + pair * 2
        m_parts, a_col_parts = [], []
        for h in (h0, h0 + 1):
            a_col = a_all[:, h:h + 1]
            a_row = at_sc[h:h + 1, :]
            seg = jnp.where(causal, a_col - a_row, -jnp.inf)
            m_parts.append(cb * jnp.exp2(seg).astype(BF16))
            a_col_parts.append(a_col)
        m_pair = jnp.concatenate(m_parts, axis=1)
        a_col_pair = jnp.where(low_half, a_col_parts[0], a_col_parts[1])
        a_tot_pair = jnp.where(low_half[0:1], a_all[edge:edge + 1, h0:h0 + 1], a_all[edge:edge + 1, h0 + 1:h0 + 2])
        y_slabs.append(jnp.dot(m_pair, rhs, preferred_element_type=F32) + jnp.exp2(a_col_pair) * y_off)
        if wide_state:
            w_pair = jnp.where(low_half, w_all[:, h0:h0 + 1], w_all[:, h0 + 1:h0 + 2])
            xw_slabs.append(xs_pair * w_pair.astype(BF16))
            tot_slabs.append(a_tot_pair)
        else:
            bw_pair = jnp.concatenate([bm_t * wdt_sc[h:h + 1, :].astype(BF16) for h in (h0, h0 + 1)], axis=1)
            st_ref[g, :, sl] = jnp.exp2(a_tot_pair) * st_pair + jnp.dot(bw_pair, rhs, preferred_element_type=F32)
    if wide_state:
        decay = jnp.exp2(jnp.concatenate(tot_slabs, axis=1))
        st_ref[g] = decay * st_ref[g] + jnp.dot(bm_t, jnp.concatenate(xw_slabs, axis=1), preferred_element_type=F32)
    return jnp.concatenate(y_slabs, axis=1), xs


def _chunk_rows(index):
    return pl.ds(pl.multiple_of(index * CHUNK, CHUNK), CHUNK)


def _ssd_fwd_kernel(xs_ref, bm_ref, cm_ref, dt_ref, dtb_ref, a_ref, y_ref, st_ref, a_sc, at_sc, wdt_sc):
    @pl.when(pl.program_id(1) == 0)
    def _():
        st_ref[...] = jnp.zeros_like(st_ref)

    def chunk(c, carry):
        rows = _chunk_rows(c)
        _ssd_chunk_prologue(dt_ref[rows, :], dtb_ref, a_ref, a_sc, at_sc, wdt_sc, reverse=False,
                            wide_state=FWD_WIDE_STATE)
        for g in range(SSM_GROUPS):
            y, _ = _ssd_group(g, rows, xs_ref, bm_ref, cm_ref, st_ref, a_sc, at_sc, wdt_sc, head0=0, reverse=False,
                              wide_state=FWD_WIDE_STATE)
            y_ref[g, rows, :] = y.astype(y_ref.dtype)
        return carry

    lax.fori_loop(0, dt_ref.shape[0] // CHUNK, chunk, 0)


def _ssd_bwd_kernel(xs_ref, bm_ref, cm_ref, dt_ref, dtb_ref, a_ref, yf_ref, z_ref, dskip_ref, nw_ref, y_ref,
                    st_ref, a_sc, at_sc, wdt_sc, *, n_heads):
    @pl.when(pl.program_id(1) == 0)
    def _():
        st_ref[...] = jnp.zeros_like(st_ref)

    n_chunks = dt_ref.shape[0] // CHUNK

    def chunk(it, carry):
        rows = _chunk_rows(n_chunks - 1 - it)
        _ssd_chunk_prologue(dt_ref[rows, :], dtb_ref, a_ref, a_sc, at_sc, wdt_sc, reverse=True,
                            wide_state=BWD_WIDE_STATE)
        for g in range(SSM_GROUPS):
            y, xs = _ssd_group(g, rows, xs_ref, bm_ref, cm_ref, st_ref, a_sc, at_sc, wdt_sc, head0=n_heads,
                               reverse=True, wide_state=BWD_WIDE_STATE)
            y = y + yf_ref[g, rows, :].astype(F32) + xs.astype(F32) * dskip_ref[g]
            y = y * _silu(z_ref[g, rows, :].astype(F32))
            y_ref[g, rows, :] = _rmsnorm(y, nw_ref[g]).astype(y_ref.dtype)
        return carry

    lax.fori_loop(0, n_chunks, chunk, 0)


def _ssd_scratch(two_h):
    return [
        pltpu.VMEM((SSM_GROUPS, STATE_DIM, COL_BLOCK), F32),
        pltpu.VMEM((CHUNK, two_h), F32),
        pltpu.VMEM((two_h, CHUNK), F32),
        pltpu.VMEM((two_h, CHUNK), F32),
    ]


def _ssd_est(two_h, n_slabs, rows):
    return (2 * n_slabs * SSM_GROUPS * rows * COL_BLOCK * 2 + 4 * SSM_GROUPS * rows * STATE_DIM * 2
            + SSM_GROUPS * STATE_DIM * COL_BLOCK * 4 + 2 * rows * two_h * 4 + 6 * CHUNK * two_h * 4 + 8 * 1024 * 1024)


def _ssd_forward(xs, bm, cm, dt_raw, dt_bias, a_neg, bsz, seq):
    g, t, _ = xs.shape
    rows = _tile(seq, SSD_STEP_CHUNKS * CHUNK)
    nc = seq // rows
    two_h = dt_raw.shape[1]
    slab = pl.BlockSpec((g, rows, COL_BLOCK), lambda b, c: (0, b * nc + c, 0))
    bc = pl.BlockSpec((g, rows, STATE_DIM), lambda b, c: (0, b * nc + c, 0))
    row = pl.BlockSpec((1, two_h), lambda b, c: (0, 0))
    return pl.pallas_call(
        _ssd_fwd_kernel,
        grid=(bsz, nc),
        in_specs=[slab, bc, bc, pl.BlockSpec((rows, two_h), lambda b, c: (b * nc + c, 0)), row, row],
        out_specs=slab,
        out_shape=jax.ShapeDtypeStruct((g, t, COL_BLOCK), BF16),
        scratch_shapes=_ssd_scratch(two_h),
        compiler_params=_params(("parallel", "arbitrary"), _ssd_est(two_h, 2, rows)),
        name="ssd_fwd",
    )(xs, bm, cm, dt_raw, dt_bias, a_neg)


def _ssd_backward(xs, bm, cm, dt_raw, dt_bias, a_neg, y_fwd, proj, d_skip, norm_w, bsz, seq, z_block):
    g, t, _ = xs.shape
    rows = _tile(seq, SSD_STEP_CHUNKS * CHUNK)
    nc = seq // rows
    two_h = dt_raw.shape[1]
    rev = lambda b, c: (0, b * nc + nc - 1 - c, 0)
    slab = pl.BlockSpec((g, rows, COL_BLOCK), rev)
    bc = pl.BlockSpec((g, rows, STATE_DIM), rev)
    row = pl.BlockSpec((1, two_h), lambda b, c: (0, 0))
    par = pl.BlockSpec((g, 1, COL_BLOCK), lambda b, c: (0, 0, 0))
    zb = z_block // g
    assert zb * g == z_block
    kern = functools.partial(_ssd_bwd_kernel, n_heads=two_h // 2)
    return pl.pallas_call(
        kern,
        grid=(bsz, nc),
        in_specs=[slab, bc, bc, pl.BlockSpec((rows, two_h), lambda b, c: (b * nc + nc - 1 - c, 0)), row, row,
                  slab, pl.BlockSpec((g, rows, COL_BLOCK), lambda b, c: (zb, b * nc + nc - 1 - c, 0)), par, par],
        out_specs=slab,
        out_shape=jax.ShapeDtypeStruct((g, t, COL_BLOCK), BF16),
        scratch_shapes=_ssd_scratch(two_h),
        compiler_params=_params(("parallel", "arbitrary"), _ssd_est(two_h, 4, rows)),
        name="ssd_bwd",
    )(xs, bm, cm, dt_raw, dt_bias, a_neg, y_fwd, proj, d_skip, norm_w)


def _t5_buckets(rel):
    half = REL_BUCKETS // 2
    ret = (rel > 0).astype(np.int32) * half
    n = np.abs(rel)
    max_exact = half // 2
    large = max_exact + (np.log(np.maximum(n, 1) / max_exact) / np.log(REL_MAX_DIST / max_exact)
                         * (half - max_exact)).astype(np.int32)
    large = np.minimum(large, half - 1)
    return ret + np.where(n < max_exact, n, large).astype(np.int32)


def _bias_kernel(rel_bias_ref, bucket_ref, o_ref):
    h = pl.program_id(0)
    bucket = bucket_ref[...]
    acc = jnp.zeros(bucket.shape, F32)
    for b in range(REL_BUCKETS):
        acc = jnp.where(bucket == b, rel_bias_ref[b, h], acc)
    o_ref[...] = jnp.where(bucket < 0, MASK_VALUE, acc * LOG2_E)


def _attention_bias(rel_bias):
    blk = ATTN_BLOCK
    rel = np.arange(3 * blk)[None, :] - blk - np.arange(blk)[:, None]
    bucket = np.where(np.abs(rel) <= WINDOW, _t5_buckets(rel), -1).astype(np.int32)
    heads = rel_bias.shape[1]
    return pl.pallas_call(
        _bias_kernel,
        grid=(heads,),
        in_specs=[pl.BlockSpec(memory_space=pltpu.SMEM), pl.BlockSpec((blk, 3 * blk), lambda h: (0, 0))],
        out_specs=pl.BlockSpec((None, blk, 3 * blk), lambda h: (h, 0, 0)),
        out_shape=jax.ShapeDtypeStruct((heads, blk, 3 * blk), F32),
        compiler_params=_params(("arbitrary",), 4 * 1024 * 1024),
        name="attn_bias",
    )(rel_bias, jnp.asarray(bucket))


def _attn_kernel(sink_ref, q_ref, k_ref, kp_ref, kn_ref, v_ref, vp_ref, vn_ref, bias_ref, o_ref, kw_ref, vw_ref,
                 *, tq, seq, scale):
    blk = ATTN_BLOCK
    d = ATTN_HEAD_DIM
    kw_ref[0:blk, :] = kp_ref[...]
    kw_ref[blk:blk + tq, :] = k_ref[...]
    kw_ref[blk + tq:, :] = kn_ref[...]
    vw_ref[0:blk, :] = vp_ref[...]
    vw_ref[blk:blk + tq, :] = v_ref[...]
    vw_ref[blk + tq:, :] = vn_ref[...]
    step0 = pl.program_id(1) * tq
    key_off = lax.broadcasted_iota(jnp.int32, (1, 3 * blk), 1) - blk
    n_sub = tq // blk
    for g in range(KV_HEADS):
        for i in range(n_sub):
            kwin = kw_ref[i * blk:(i + 3) * blk, g * d:(g + 1) * d]
            vwin = vw_ref[i * blk:(i + 3) * blk, g * d:(g + 1) * d]
            q_rows = q_ref[g, i * blk:(i + 1) * blk, :]
            q4 = jnp.concatenate([q_rows[:, r * d:(r + 1) * d] for r in range(Q_PER_KV)], axis=0)
            logits = lax.dot_general(q4, kwin, (((1,), (1,)), ((), ())), preferred_element_type=F32)
            logits = logits * (scale * LOG2_E) + bias_ref[g]
            if i == 0 or i == n_sub - 1:
                key_pos = step0 + i * blk + key_off
                logits = jnp.where((key_pos >= 0) & (key_pos < seq), logits, MASK_VALUE)
            probs, inv = [], []
            for r in range(Q_PER_KV):
                lg = logits[r * blk:(r + 1) * blk]
                sink = sink_ref[g * Q_PER_KV + r] * LOG2_E
                m = jnp.maximum(jnp.max(lg, axis=-1, keepdims=True), sink)
                p = jnp.exp2(lg - m)
                inv.append(1.0 / (jnp.sum(p, axis=-1, keepdims=True) + jnp.exp2(sink - m)))
                probs.append(p.astype(BF16))
            out = jnp.dot(jnp.concatenate(probs, axis=0), vwin, preferred_element_type=F32)
            for r in range(Q_PER_KV):
                o_ref[g, i * blk:(i + 1) * blk, r * d:(r + 1) * d] = (
                    out[r * blk:(r + 1) * blk] * inv[r]).astype(o_ref.dtype)


def _attention(proj, bias, sink, bsz, seq, q_block, k_block, v_block):
    t = proj.shape[1]
    blk = ATTN_BLOCK
    tq = _tile(seq, 512)
    nq = seq // tq
    per = tq // blk
    last = t // blk - 1
    width = Q_PER_KV * ATTN_HEAD_DIM
    assert width == COL_BLOCK and KV_HEADS * ATTN_HEAD_DIM == COL_BLOCK
    qb = q_block // KV_HEADS
    assert qb * KV_HEADS == q_block
    main = lambda cb: pl.BlockSpec((None, tq, COL_BLOCK), lambda b, i: (cb, b * nq + i, 0))
    prev = lambda cb: pl.BlockSpec((None, blk, COL_BLOCK), lambda b, i: (cb, jnp.maximum((b * nq + i) * per - 1, 0), 0))
    nxt = lambda cb: pl.BlockSpec((None, blk, COL_BLOCK), lambda b, i: (cb, jnp.minimum((b * nq + i + 1) * per, last), 0))
    kern = functools.partial(_attn_kernel, tq=tq, seq=seq, scale=ATTN_HEAD_DIM ** -0.5)
    est = (2 * KV_HEADS * tq * COL_BLOCK * 2 * 2 + 4 * (tq + 2 * blk) * COL_BLOCK * 2 + 2 * (tq + 2 * blk) * COL_BLOCK * 2
           + 2 * KV_HEADS * Q_PER_KV * blk * 3 * blk * 4 + 8 * 1024 * 1024)
    window = pltpu.VMEM((tq + 2 * blk, COL_BLOCK), BF16)
    return pl.pallas_call(
        kern,
        grid=(bsz, nq),
        in_specs=[
            pl.BlockSpec(memory_space=pltpu.SMEM),
            pl.BlockSpec((KV_HEADS, tq, COL_BLOCK), lambda b, i: (qb, b * nq + i, 0)),
            main(k_block), prev(k_block), nxt(k_block),
            main(v_block), prev(v_block), nxt(v_block),
            pl.BlockSpec((KV_HEADS, Q_PER_KV * blk, 3 * blk), lambda b, i: (0, 0, 0)),
        ],
        out_specs=pl.BlockSpec((KV_HEADS, tq, COL_BLOCK), lambda b, i: (0, b * nq + i, 0)),
        out_shape=jax.ShapeDtypeStruct((KV_HEADS, t, COL_BLOCK), BF16),
        scratch_shapes=[window, window],
        compiler_params=_params(("parallel", "parallel"), est),
        name="attention",
    )(sink, proj, proj, proj, proj, proj, proj, proj, bias)


def _merge_kernel(ys_ref, at_ref, ga_ref, gb_ref, ws_ref, wa_ref, o_ref):
    def branch(lhs_ref, w_ref):
        acc = None
        for c in range(lhs_ref.shape[0]):
            part = jnp.dot(lhs_ref[c], w_ref[c * COL_BLOCK:(c + 1) * COL_BLOCK, :], preferred_element_type=F32)
            acc = part if acc is None else acc + part
        return acc

    ya = branch(ys_ref, ws_ref)
    yb = branch(at_ref, wa_ref)
    merged = _sigmoid(ga_ref[...].astype(F32)) * ya + _sigmoid(gb_ref[...].astype(F32)) * yb
    o_ref[...] = merged.astype(o_ref.dtype)


def _merge(y_ssm, attn, proj, w_ssm, w_attn, gate_block):
    ks, t, _ = y_ssm.shape
    ka = attn.shape[0]
    dm = w_ssm.shape[1]
    nb = dm // COL_BLOCK
    tm = _tile(t, 1024)
    est = 2 * (ks + ka) * tm * COL_BLOCK * 2 + 2 * (ks + ka) * COL_BLOCK * COL_BLOCK * 2 + 6 * tm * COL_BLOCK * 2 + 3 * tm * COL_BLOCK * 4
    return pl.pallas_call(
        _merge_kernel,
        grid=(t // tm, nb),
        in_specs=[
            pl.BlockSpec((ks, tm, COL_BLOCK), lambda i, j: (0, i, 0)),
            pl.BlockSpec((ka, tm, COL_BLOCK), lambda i, j: (0, i, 0)),
            pl.BlockSpec((None, tm, COL_BLOCK), lambda i, j: (gate_block + _serpentine(i, j, nb), i, 0)),
            pl.BlockSpec((None, tm, COL_BLOCK), lambda i, j: (gate_block + nb + _serpentine(i, j, nb), i, 0)),
            pl.BlockSpec((ks * COL_BLOCK, COL_BLOCK), lambda i, j: (0, _serpentine(i, j, nb))),
            pl.BlockSpec((ka * COL_BLOCK, COL_BLOCK), lambda i, j: (0, _serpentine(i, j, nb))),
        ],
        out_specs=pl.BlockSpec((None, tm, COL_BLOCK), lambda i, j: (_serpentine(i, j, nb), i, 0)),
        out_shape=jax.ShapeDtypeStruct((nb, t, COL_BLOCK), BF16),
        compiler_params=_params(("parallel", "arbitrary"), est),
        name="merge",
    )(y_ssm, attn, proj, proj, w_ssm, w_attn)


def _outproj_kernel(m_ref, w_ref, x_ref, o_ref):
    acc = x_ref[...]
    for c in range(m_ref.shape[0]):
        acc = acc + jnp.dot(m_ref[c], w_ref[c * COL_BLOCK:(c + 1) * COL_BLOCK, :], preferred_element_type=F32)
    o_ref[...] = acc


def _out_proj(merged, w_out, x):
    kb, t, _ = merged.shape
    d = w_out.shape[1]
    tm = _tile(t, 512)
    tn = d
    est = 2 * kb * tm * COL_BLOCK * 2 + 2 * kb * COL_BLOCK * tn * 2 + 4 * tm * tn * 4 + tm * tn * 4
    return pl.pallas_call(
        _outproj_kernel,
        grid=(t // tm, d // tn),
        in_specs=[
            pl.BlockSpec((kb, tm, COL_BLOCK), lambda i, j: (0, i, 0)),
            pl.BlockSpec((kb * COL_BLOCK, tn), lambda i, j: (0, j)),
            pl.BlockSpec((tm, tn), lambda i, j: (i, j)),
        ],
        out_specs=pl.BlockSpec((tm, tn), lambda i, j: (i, j)),
        out_shape=jax.ShapeDtypeStruct((t, d), F32),
        compiler_params=_params(("parallel", "arbitrary"), est),
        name="out_proj",
    )(merged, w_out, x)


def _ffn_kernel(h_ref, nw_ref, wg_ref, wu_ref, wo_ref, fw_ref, o_ref, hn_ref, acc_ref, *, final_norm):
    f = pl.program_id(1)

    @pl.when(f == 0)
    def _():
        hn_ref[...] = _rmsnorm(h_ref[...], nw_ref[...]).astype(BF16)
        acc_ref[...] = jnp.zeros_like(acc_ref)

    hn = hn_ref[...]
    tf = wg_ref.shape[1]
    part = tf // FFN_SPLIT
    acc = acc_ref[...]
    for s in range(FFN_SPLIT):
        cols = slice(s * part, (s + 1) * part)
        gate = jnp.dot(hn, wg_ref[:, cols], preferred_element_type=F32)
        up = jnp.dot(hn, wu_ref[:, cols], preferred_element_type=F32)
        act = (_silu(gate) * up).astype(BF16)
        acc = acc + jnp.dot(act, wo_ref[cols, :], preferred_element_type=F32)
    acc_ref[...] = acc

    @pl.when(f == pl.num_programs(1) - 1)
    def _():
        out = h_ref[...] + acc_ref[...]
        o_ref[...] = _rmsnorm(out, fw_ref[...]) if final_norm else out


def _ffn(h, norm_w, w_in, w_out, final_w, final_norm):
    t, d = h.shape
    ff = w_out.shape[0]
    tm = _tile(t, 512)
    tf = _tile(ff, 512)
    nf = ff // tf
    est = 2 * tm * d * 4 * 2 + tm * d * 2 + tm * d * 4 + 2 * 3 * d * tf * 2 + 4 * tm * tf * 4
    return pl.pallas_call(
        functools.partial(_ffn_kernel, final_norm=final_norm),
        grid=(t // tm, nf),
        in_specs=[
            pl.BlockSpec((tm, d), lambda i, f: (i, 0)),
            pl.BlockSpec((1, d), lambda i, f: (0, 0)),
            pl.BlockSpec((d, tf), lambda i, f: (0, _serpentine(i, f, nf))),
            pl.BlockSpec((d, tf), lambda i, f: (0, nf + _serpentine(i, f, nf))),
            pl.BlockSpec((tf, d), lambda i, f: (_serpentine(i, f, nf), 0)),
            pl.BlockSpec((1, d), lambda i, f: (0, 0)),
        ],
        out_specs=pl.BlockSpec((tm, d), lambda i, f: (i, 0)),
        out_shape=jax.ShapeDtypeStruct((t, d), F32),
        scratch_shapes=[pltpu.VMEM((tm, d), BF16), pltpu.VMEM((tm, d), F32)],
        compiler_params=_params(("parallel", "arbitrary"), est),
        name="ffn",
    )(h, norm_w.reshape(1, d), w_in, w_in, w_out, final_w.reshape(1, d))


def _layer_weights(w_in, conv_w, conv_b, dt_bias, a_log, d_skip, ssm_norm_w, d_model):
    d_inner = ssm_norm_w.shape[0]
    n_heads = d_skip.shape[0]
    gn = SSM_GROUPS * STATE_DIM
    conv_dim = d_inner + 2 * gn
    q_dim = d_model
    kv_dim = KV_HEADS * ATTN_HEAD_DIM
    cuts = np.cumsum([d_inner, conv_dim, 2 * n_heads, q_dim, kv_dim, kv_dim, 2 * d_model])
    z_w, xbc_w, dt_w, q_w, k_w, v_w, gate_w = jnp.split(w_in, cuts[:-1], axis=1)
    w_main = jnp.concatenate([z_w, q_w, xbc_w, gate_w, k_w, v_w], axis=1).astype(BF16)
    blocks = {}
    off = 0
    for name, width in (("z", d_inner), ("q", q_dim), ("xbc", conv_dim), ("gate", 2 * d_model), ("k", kv_dim), ("v", kv_dim)):
        blocks[name] = off // COL_BLOCK
        off += width
    n_conv = conv_dim // COL_BLOCK
    conv_w3 = conv_w.reshape(CONV_WIDTH, n_conv, COL_BLOCK).transpose(1, 0, 2)
    conv_b3 = conv_b.reshape(n_conv, 1, COL_BLOCK)
    dtb = dt_bias.reshape(1, 2 * n_heads)
    a_neg = (-jnp.exp(a_log.astype(F32))).reshape(1, 2 * n_heads)
    dskip = jnp.repeat(d_skip.astype(F32), HEAD_DIM_SSM).reshape(SSM_GROUPS, 1, COL_BLOCK)
    nw = ssm_norm_w.astype(F32).reshape(SSM_GROUPS, 1, COL_BLOCK)
    return w_main, dt_w.astype(BF16), blocks, conv_w3, conv_b3, dtb, a_neg, dskip, nw


def _trunk(x, mix_norm_w, w_in, conv_w, conv_b, dt_bias, a_log, d_skip, ssm_norm_w, w_ssm_branch, rel_bias,
           attn_sink, w_attn_branch, w_out, ffn_norm_w, w_ffn_in, w_ffn_out, final_norm_w):
    bsz, seq, d_model = x.shape
    depth = w_in.shape[0]
    h = x.reshape(bsz * seq, d_model)
    bias = _attention_bias(rel_bias).reshape(KV_HEADS, Q_PER_KV * ATTN_BLOCK, 3 * ATTN_BLOCK)
    for l in range(depth):
        w_main, w_dt, blocks, conv_w3, conv_b3, dtb, a_neg, dskip, nw = _layer_weights(
            w_in[l], conv_w[l], conv_b[l], dt_bias[l], a_log[l], d_skip[l], ssm_norm_w[l], d_model)
        proj, dt_raw = _in_proj(h, mix_norm_w[l], w_main, w_dt)
        n_xs = ssm_norm_w.shape[1] // COL_BLOCK
        xs, bm, cm = _conv_silu(proj, conv_w3, conv_b3, bsz, seq, blocks["xbc"], n_xs)
        y_fwd = _ssd_forward(xs, bm, cm, dt_raw, dtb, a_neg, bsz, seq)
        y_ssm = _ssd_backward(xs, bm, cm, dt_raw, dtb, a_neg, y_fwd, proj, dskip, nw, bsz, seq, blocks["z"])
        attn = _attention(proj, bias, attn_sink[l], bsz, seq, blocks["q"], blocks["k"], blocks["v"])
        merged = _merge(y_ssm, attn, proj, w_ssm_branch[l].astype(BF16), w_attn_branch[l].astype(BF16), blocks["gate"])
        h = _out_proj(merged, w_out[l].astype(BF16), h)
        h = _ffn(h, ffn_norm_w[l], w_ffn_in[l].astype(BF16), w_ffn_out[l].astype(BF16), final_norm_w,
                 final_norm=l == depth - 1)
    return h.reshape(bsz, seq, d_model)


def kernel(x_prompt, x_sample, mix_norm_w, w_in, conv_w, conv_b, dt_bias, a_log, d_skip, ssm_norm_w, w_ssm_branch,
           rel_bias, attn_sink, w_attn_branch, w_out, ffn_norm_w, w_ffn_in, w_ffn_out, final_norm_w):
    args = (mix_norm_w, w_in, conv_w, conv_b, dt_bias, a_log, d_skip, ssm_norm_w, w_ssm_branch, rel_bias, attn_sink,
            w_attn_branch, w_out, ffn_norm_w, w_ffn_in, w_ffn_out, final_norm_w)
    return (_trunk(x_prompt, *args), _trunk(x_sample, *args))
```

```python
import functools
import math

import numpy as np
import jax
import jax.numpy as jnp
from jax import lax
from jax.experimental import pallas as pl
from jax.experimental.pallas import tpu as pltpu

F32 = jnp.float32
BF16 = jnp.bfloat16

EPS = 1e-6
HEAD_DIM_SSM = 64
STATE_DIM = 128
SSM_GROUPS = 8
CHUNK = 128
FFN_SPLIT = 2
SSD_STEP_CHUNKS = 4
CONV_WIDTH = 5
ATTN_HEAD_DIM = 128
KV_HEADS = 4
Q_PER_KV = 4
WINDOW = 128
ATTN_BLOCK = 128
REL_BUCKETS = 32
REL_MAX_DIST = 128
COL_BLOCK = 512
SUBLANES = 8
HALO_ROWS = 16
CONV_ROWS = 128
CONV_PAD = 64
LOG2_E = math.log2(math.e)
MASK_VALUE = -1e30
VMEM_CAP = 60000 * 1024


def _vmem_limit(estimate_bytes):
    return int(min(VMEM_CAP, max(estimate_bytes * 5 // 4, 16 * 1024 * 1024)))


def _params(semantics, vmem_estimate):
    return pltpu.CompilerParams(dimension_semantics=semantics,
                                vmem_limit_bytes=_vmem_limit(vmem_estimate))


def _sigmoid(x):
    return 1.0 / (1.0 + jnp.exp2(x * -LOG2_E))


def _silu(x):
    return x * _sigmoid(x)


def _softplus(x):
    return jnp.maximum(x, 0.0) + jnp.log1p(jnp.exp(-jnp.abs(x)))


def _rmsnorm(x, w):
    ms = jnp.mean(x * x, axis=-1, keepdims=True)
    return x * lax.rsqrt(ms + EPS) * w


def _tile(n, want):
    t = min(n, want)
    while n % t:
        t //= 2
    return t


def _serpentine(i, j, n):
    return jnp.where(i % 2 == 0, j, n - 1 - j)


def _inproj_kernel(x_ref, nw_ref, w_ref, wdt_ref, o_ref, dt_ref, xn_ref):
    @pl.when(pl.program_id(1) == 0)
    def _():
        xn = _rmsnorm(x_ref[...], nw_ref[...]).astype(BF16)
        xn_ref[...] = xn
        dt_ref[...] = jnp.dot(xn, wdt_ref[...], preferred_element_type=F32)

    out = jnp.dot(xn_ref[...], w_ref[...], preferred_element_type=F32).astype(o_ref.dtype)
    for c in range(o_ref.shape[0]):
        o_ref[c] = out[:, c * COL_BLOCK:(c + 1) * COL_BLOCK]


def _in_proj(x, norm_w, w_main, w_dt):
    t, d = x.shape
    n = w_main.shape[1]
    nb = n // COL_BLOCK
    tm = _tile(t, 1024)
    per = 2 if nb % 2 == 0 else 1
    tn = per * COL_BLOCK
    ndt = w_dt.shape[1]
    nj = nb // per
    est = 2 * tm * d * 4 + tm * d * 2 + 2 * d * tn * 2 + 2 * tm * tn * 2 + tm * tn * 4 + 2 * d * ndt * 2 + 2 * tm * ndt * 4
    return pl.pallas_call(
        _inproj_kernel,
        grid=(t // tm, nj),
        in_specs=[
            pl.BlockSpec((tm, d), lambda i, j: (i, 0)),
            pl.BlockSpec((1, d), lambda i, j: (0, 0)),
            pl.BlockSpec((d, tn), lambda i, j: (0, _serpentine(i, j, nj))),
            pl.BlockSpec((d, ndt), lambda i, j: (0, 0)),
        ],
        out_specs=[
            pl.BlockSpec((per, tm, COL_BLOCK), lambda i, j: (_serpentine(i, j, nj), i, 0)),
            pl.BlockSpec((tm, ndt), lambda i, j: (i, 0)),
        ],
        out_shape=[
            jax.ShapeDtypeStruct((nb, t, COL_BLOCK), BF16),
            jax.ShapeDtypeStruct((t, ndt), F32),
        ],
        scratch_shapes=[pltpu.VMEM((tm, d), BF16)],
        compiler_params=_params(("parallel", "arbitrary"), est),
        name="in_proj",
    )(x, norm_w.reshape(1, d), w_main, w_dt)


def _shift_matrix():
    half = CONV_WIDTH // 2
    n_shift = CONV_WIDTH - 1
    rows = np.arange(n_shift * CONV_ROWS)[:, None]
    cols = np.arange(CONV_ROWS + 2 * CONV_PAD)[None, :]
    q, rem = rows // (n_shift * SUBLANES), rows % (n_shift * SUBLANES)
    blk, t = rem // SUBLANES, q * SUBLANES + rem % SUBLANES
    off = blk - half + (blk >= half)
    return jnp.asarray(cols == t + CONV_PAD + off, BF16)


def _conv_kernel(main_ref, prev_ref, next_ref, w_ref, b_ref, shift_ref, xs_ref, bm_ref, cm_ref, win_ref, *, tr, n_xs):
    i = pl.program_id(1)
    last = pl.num_programs(1) - 1
    half = CONV_WIDTH // 2
    n_blocks = main_ref.shape[0]
    shift = shift_ref[...]
    win_ref[0:CONV_PAD - HALO_ROWS, :] = jnp.zeros((CONV_PAD - HALO_ROWS, COL_BLOCK), BF16)
    win_ref[CONV_PAD + tr + HALO_ROWS:, :] = jnp.zeros((CONV_PAD - HALO_ROWS, COL_BLOCK), BF16)

    def conv_block(c):
        halo_zero = jnp.zeros((HALO_ROWS, COL_BLOCK), BF16)
        win_ref[CONV_PAD - HALO_ROWS:CONV_PAD, :] = jnp.where(i > 0, prev_ref[c], halo_zero)
        win_ref[CONV_PAD:CONV_PAD + tr, :] = main_ref[c]
        win_ref[CONV_PAD + tr:CONV_PAD + tr + HALO_ROWS, :] = jnp.where(i < last, next_ref[c], halo_zero)
        w = w_ref[c]
        bias = b_ref[c]
        outs = []
        for r in range(tr // CONV_ROWS):
            xwin = win_ref[r * CONV_ROWS:(r + 1) * CONV_ROWS + 2 * CONV_PAD, :]
            taps = jnp.dot(shift, xwin, preferred_element_type=F32)
            centre = xwin[CONV_PAD:CONV_PAD + CONV_ROWS].astype(F32)
            groups = []
            for q in range(CONV_ROWS // SUBLANES):
                acc = bias + centre[q * SUBLANES:(q + 1) * SUBLANES] * w[half:half + 1]
                for b in range(CONV_WIDTH - 1):
                    j = b + (b >= half)
                    row0 = (q * (CONV_WIDTH - 1) + b) * SUBLANES
                    acc = acc + taps[row0:row0 + SUBLANES] * w[j:j + 1]
                groups.append(acc)
            outs.append(_silu(jnp.concatenate(groups, axis=0)))
        return outs

    def xs_body(c, carry):
        for r, act in enumerate(conv_block(c)):
            xs_ref[c, r * CONV_ROWS:(r + 1) * CONV_ROWS, :] = act.astype(xs_ref.dtype)
        return carry

    lax.fori_loop(0, n_xs, xs_body, 0)
    per = COL_BLOCK // STATE_DIM
    n_b = (n_blocks - n_xs) // 2

    def state_body(first, dst):
        def body(c, carry):
            for r, act in enumerate(conv_block(c)):
                act = act.astype(dst.dtype)
                for k in range(per):
                    dst[(c - first) * per + k, r * CONV_ROWS:(r + 1) * CONV_ROWS, :] = (
                        act[:, k * STATE_DIM:(k + 1) * STATE_DIM])
            return carry
        return body

    lax.fori_loop(n_xs, n_xs + n_b, state_body(n_xs, bm_ref), 0)
    lax.fori_loop(n_xs + n_b, n_blocks, state_body(n_xs + n_b, cm_ref), 0)


def _conv_silu(proj, conv_w, conv_b, bsz, seq, first_block, n_xs):
    t = proj.shape[1]
    n_blocks = conv_w.shape[0]
    tr = _tile(seq, 512)
    nr = seq // tr
    hb = tr // HALO_ROWS
    last_halo = t // HALO_ROWS - 1
    n_bc = (n_blocks - n_xs) // 2 * (COL_BLOCK // STATE_DIM)
    cb0 = first_block // n_blocks
    assert cb0 * n_blocks == first_block
    assert tr % CONV_ROWS == 0
    est = (2 * n_blocks * (tr + 2 * HALO_ROWS) * COL_BLOCK * 2 + 2 * (n_xs * tr * COL_BLOCK + 2 * n_bc * tr * STATE_DIM) * 2
           + (tr + 2 * CONV_PAD) * COL_BLOCK * 2 + 4 * n_blocks * 8 * COL_BLOCK * 4 + 8 * 1024 * 1024)
    shift = _shift_matrix()
    kern = functools.partial(_conv_kernel, tr=tr, n_xs=n_xs)
    return pl.pallas_call(
        kern,
        grid=(bsz, nr),
        in_specs=[
            pl.BlockSpec((n_blocks, tr, COL_BLOCK), lambda b, i: (cb0, b * nr + i, 0)),
            pl.BlockSpec((n_blocks, HALO_ROWS, COL_BLOCK),
                         lambda b, i: (cb0, jnp.maximum((b * nr + i) * hb - 1, 0), 0)),
            pl.BlockSpec((n_blocks, HALO_ROWS, COL_BLOCK),
                         lambda b, i: (cb0, jnp.minimum((b * nr + i + 1) * hb, last_halo), 0)),
            pl.BlockSpec((n_blocks, CONV_WIDTH, COL_BLOCK), lambda b, i: (0, 0, 0)),
            pl.BlockSpec((n_blocks, 1, COL_BLOCK), lambda b, i: (0, 0, 0)),
            pl.BlockSpec(shift.shape, lambda b, i: (0, 0)),
        ],
        out_specs=[
            pl.BlockSpec((n_xs, tr, COL_BLOCK), lambda b, i: (0, b * nr + i, 0)),
            pl.BlockSpec((n_bc, tr, STATE_DIM), lambda b, i: (0, b * nr + i, 0)),
            pl.BlockSpec((n_bc, tr, STATE_DIM), lambda b, i: (0, b * nr + i, 0)),
        ],
        out_shape=[
            jax.ShapeDtypeStruct((n_xs, t, COL_BLOCK), BF16),
            jax.ShapeDtypeStruct((n_bc, t, STATE_DIM), BF16),
            jax.ShapeDtypeStruct((n_bc, t, STATE_DIM), BF16),
        ],
        scratch_shapes=[pltpu.VMEM((tr + 2 * CONV_PAD, COL_BLOCK), BF16)],
        compiler_params=_params(("parallel", "parallel"), est),
        name="conv_silu",
    )(proj, proj, proj, conv_w, conv_b, shift)


def _scan_rows(v, reverse):
    n = v.shape[0]
    row = lax.broadcasted_iota(jnp.int32, v.shape, 0)
    k = 1
    while k < n:
        if reverse:
            shifted = pltpu.roll(v, n - k, axis=0)
            keep = row < n - k
        else:
            shifted = pltpu.roll(v, k, axis=0)
            keep = row >= k
        v = v + jnp.where(keep, shifted, 0.0)
        k *= 2
    return v


def _ssd_chunk_prologue(dt_raw, dtb_ref, a_ref, a_sc, at_sc, wdt_sc, reverse):
    dt = _softplus(dt_raw + dtb_ref[...])
    acc = _scan_rows(dt * a_ref[...], reverse)
    a_sc[...] = acc * LOG2_E
    acc_t = acc.T
    dt_t = dt.T
    at_sc[...] = acc_t * LOG2_E - jnp.log2(dt_t)
    n = acc.shape[0]
    total = acc_t[:, 0:1] if reverse else acc_t[:, n - 1:n]
    wdt_sc[...] = jnp.exp(total - acc_t) * dt_t


def _ssd_group(g, rows, xs_ref, bm_ref, cm_ref, st_ref, a_sc, at_sc, wdt_sc, head0, reverse):
    l = CHUNK
    xs = xs_ref[g, rows, :]
    bm = bm_ref[g, rows, :]
    cm = cm_ref[g, rows, :]
    cb = lax.dot_general(cm, bm, (((1,), (1,)), ((), ())), preferred_element_type=F32).astype(BF16)
    bm_t = bm.astype(F32).T.astype(BF16)

    heads_per_group = COL_BLOCK // HEAD_DIM_SSM
    first = head0 + g * heads_per_group
    a_all = a_sc[...]
    row_i = lax.broadcasted_iota(jnp.int32, (l, l), 0)
    col_i = lax.broadcasted_iota(jnp.int32, (l, l), 1)
    causal = (row_i <= col_i) if reverse else (row_i >= col_i)
    lane = lax.broadcasted_iota(jnp.int32, (l, 2 * HEAD_DIM_SSM), 1)
    low_half = lane < HEAD_DIM_SSM
    keep_low = jnp.where(low_half, 1.0, 0.0).astype(BF16)
    keep_high = jnp.where(low_half, 0.0, 1.0).astype(BF16)
    edge = 0 if reverse else l - 1

    y_slabs = []
    for pair in range(heads_per_group // 2):
        sl = slice(pair * 2 * HEAD_DIM_SSM, (pair + 1) * 2 * HEAD_DIM_SSM)
        st_pair = st_ref[g, :, sl]
        y_off = jnp.dot(cm, st_pair.astype(BF16), preferred_element_type=F32)
        xs_pair = xs[:, sl]
        rhs = jnp.concatenate([xs_pair * keep_low, xs_pair * keep_high], axis=0)
        m_parts, bw_parts, a_col_parts, a_tot_parts = [], [], [], []
        for sub in range(2):
            h = first + pair * 2 + sub
            a_col = a_all[:, h:h + 1]
            a_row = at_sc[h:h + 1, :]
            wd_row = wdt_sc[h:h + 1, :]
            seg = jnp.where(causal, a_col - a_row, -jnp.inf)
            m_parts.append(cb * jnp.exp2(seg).astype(BF16))
            bw_parts.append(bm_t * wd_row.astype(BF16))
            a_col_parts.append(a_col)
            a_tot_parts.append(a_all[edge:edge + 1, h:h + 1])
        m_pair = jnp.concatenate(m_parts, axis=1)
        bw_pair = jnp.concatenate(bw_parts, axis=1)
        a_col_pair = jnp.where(low_half, a_col_parts[0], a_col_parts[1])
        a_tot_pair = jnp.where(low_half[0:1], a_tot_parts[0], a_tot_parts[1])
        y_slabs.append(jnp.dot(m_pair, rhs, preferred_element_type=F32) + jnp.exp2(a_col_pair) * y_off)
        st_ref[g, :, sl] = jnp.exp2(a_tot_pair) * st_pair + jnp.dot(bw_pair, rhs, preferred_element_type=F32)
    return jnp.concatenate(y_slabs, axis=1), xs


def _chunk_rows(index):
    return pl.ds(pl.multiple_of(index * CHUNK, CHUNK), CHUNK)


def _ssd_fwd_kernel(xs_ref, bm_ref, cm_ref, dt_ref, dtb_ref, a_ref, dskip_ref, y_ref, st_ref, a_sc, at_sc, wdt_sc):
    @pl.when(pl.program_id(1) == 0)
    def _():
        st_ref[...] = jnp.zeros_like(st_ref)

    def chunk(c, carry):
        rows = _chunk_rows(c)
        _ssd_chunk_prologue(dt_ref[rows, :], dtb_ref, a_ref, a_sc, at_sc, wdt_sc, reverse=False)
        for g in range(SSM_GROUPS):
            y, xs = _ssd_group(g, rows, xs_ref, bm_ref, cm_ref, st_ref, a_sc, at_sc, wdt_sc, head0=0, reverse=False)
            y_ref[g, rows, :] = (y + xs.astype(F32) * dskip_ref[g]).astype(y_ref.dtype)
        return carry

    lax.fori_loop(0, dt_ref.shape[0] // CHUNK, chunk, 0)


def _ssd_bwd_kernel(xs_ref, bm_ref, cm_ref, dt_ref, dtb_ref, a_ref, yf_ref, z_ref, nw_ref, y_ref,
                    st_ref, a_sc, at_sc, wdt_sc, *, n_heads):
    @pl.when(pl.program_id(1) == 0)
    def _():
        st_ref[...] = jnp.zeros_like(st_ref)

    n_chunks = dt_ref.shape[0] // CHUNK

    def chunk(it, carry):
        rows = _chunk_rows(n_chunks - 1 - it)
        _ssd_chunk_prologue(dt_ref[rows, :], dtb_ref, a_ref, a_sc, at_sc, wdt_sc, reverse=True)
        for g in range(SSM_GROUPS):
            y, _ = _ssd_group(g, rows, xs_ref, bm_ref, cm_ref, st_ref, a_sc, at_sc, wdt_sc, head0=n_heads,
                              reverse=True)
            y = y + yf_ref[g, rows, :].astype(F32)
            y = y * _silu(z_ref[g, rows, :].astype(F32))
            y_ref[g, rows, :] = _rmsnorm(y, nw_ref[g]).astype(y_ref.dtype)
        return carry

    lax.fori_loop(0, n_chunks, chunk, 0)


def _ssd_scratch(two_h):
    return [
        pltpu.VMEM((SSM_GROUPS, STATE_DIM, COL_BLOCK), F32),
        pltpu.VMEM((CHUNK, two_h), F32),
        pltpu.VMEM((two_h, CHUNK), F32),
        pltpu.VMEM((two_h, CHUNK), F32),
    ]


def _ssd_est(two_h, n_slabs, rows):
    return (2 * n_slabs * SSM_GROUPS * rows * COL_BLOCK * 2 + 4 * SSM_GROUPS * rows * STATE_DIM * 2
            + SSM_GROUPS * STATE_DIM * COL_BLOCK * 4 + 2 * rows * two_h * 4 + 6 * CHUNK * two_h * 4 + 8 * 1024 * 1024)


def _ssd_forward(xs, bm, cm, dt_raw, dt_bias, a_neg, d_skip, bsz, seq):
    g, t, _ = xs.shape
    rows = _tile(seq, SSD_STEP_CHUNKS * CHUNK)
    nc = seq // rows
    two_h = dt_raw.shape[1]
    slab = pl.BlockSpec((g, rows, COL_BLOCK), lambda b, c: (0, b * nc + c, 0))
    bc = pl.BlockSpec((g, rows, STATE_DIM), lambda b, c: (0, b * nc + c, 0))
    row = pl.BlockSpec((1, two_h), lambda b, c: (0, 0))
    par = pl.BlockSpec((g, 1, COL_BLOCK), lambda b, c: (0, 0, 0))
    return pl.pallas_call(
        _ssd_fwd_kernel,
        grid=(bsz, nc),
        in_specs=[slab, bc, bc, pl.BlockSpec((rows, two_h), lambda b, c: (b * nc + c, 0)), row, row, par],
        out_specs=slab,
        out_shape=jax.ShapeDtypeStruct((g, t, COL_BLOCK), BF16),
        scratch_shapes=_ssd_scratch(two_h),
        compiler_params=_params(("parallel", "arbitrary"), _ssd_est(two_h, 2, rows)),
        name="ssd_fwd",
    )(xs, bm, cm, dt_raw, dt_bias, a_neg, d_skip)


def _ssd_backward(xs, bm, cm, dt_raw, dt_bias, a_neg, y_fwd, proj, norm_w, bsz, seq, z_block):
    g, t, _ = xs.shape
    rows = _tile(seq, SSD_STEP_CHUNKS * CHUNK)
    nc = seq // rows
    two_h = dt_raw.shape[1]
    rev = lambda b, c: (0, b * nc + nc - 1 - c, 0)
    slab = pl.BlockSpec((g, rows, COL_BLOCK), rev)
    bc = pl.BlockSpec((g, rows, STATE_DIM), rev)
    row = pl.BlockSpec((1, two_h), lambda b, c: (0, 0))
    par = pl.BlockSpec((g, 1, COL_BLOCK), lambda b, c: (0, 0, 0))
    zb = z_block // g
    assert zb * g == z_block
    kern = functools.partial(_ssd_bwd_kernel, n_heads=two_h // 2)
    return pl.pallas_call(
        kern,
        grid=(bsz, nc),
        in_specs=[slab, bc, bc, pl.BlockSpec((rows, two_h), lambda b, c: (b * nc + nc - 1 - c, 0)), row, row,
                  slab, pl.BlockSpec((g, rows, COL_BLOCK), lambda b, c: (zb, b * nc + nc - 1 - c, 0)), par],
        out_specs=slab,
        out_shape=jax.ShapeDtypeStruct((g, t, COL_BLOCK), BF16),
        scratch_shapes=_ssd_scratch(two_h),
        compiler_params=_params(("parallel", "arbitrary"), _ssd_est(two_h, 4, rows)),
        name="ssd_bwd",
    )(xs, bm, cm, dt_raw, dt_bias, a_neg, y_fwd, proj, norm_w)


def _t5_buckets(rel):
    half = REL_BUCKETS // 2
    ret = (rel > 0).astype(np.int32) * half
    n = np.abs(rel)
    max_exact = half // 2
    large = max_exact + (np.log(np.maximum(n, 1) / max_exact) / np.log(REL_MAX_DIST / max_exact)
                         * (half - max_exact)).astype(np.int32)
    large = np.minimum(large, half - 1)
    return ret + np.where(n < max_exact, n, large).astype(np.int32)


def _bias_kernel(rel_bias_ref, bucket_ref, o_ref):
    h = pl.program_id(0)
    bucket = bucket_ref[...]
    acc = jnp.zeros(bucket.shape, F32)
    for b in range(REL_BUCKETS):
        acc = jnp.where(bucket == b, rel_bias_ref[b, h], acc)
    o_ref[...] = jnp.where(bucket < 0, MASK_VALUE, acc * LOG2_E)


def _attention_bias(rel_bias):
    blk = ATTN_BLOCK
    rel = np.arange(3 * blk)[None, :] - blk - np.arange(blk)[:, None]
    bucket = np.where(np.abs(rel) <= WINDOW, _t5_buckets(rel), -1).astype(np.int32)
    heads = rel_bias.shape[1]
    return pl.pallas_call(
        _bias_kernel,
        grid=(heads,),
        in_specs=[pl.BlockSpec(memory_space=pltpu.SMEM), pl.BlockSpec((blk, 3 * blk), lambda h: (0, 0))],
        out_specs=pl.BlockSpec((None, blk, 3 * blk), lambda h: (h, 0, 0)),
        out_shape=jax.ShapeDtypeStruct((heads, blk, 3 * blk), F32),
        compiler_params=_params(("arbitrary",), 4 * 1024 * 1024),
        name="attn_bias",
    )(rel_bias, jnp.asarray(bucket))


def _attn_kernel(sink_ref, q_ref, k_ref, kp_ref, kn_ref, v_ref, vp_ref, vn_ref, bias_ref, o_ref, kw_ref, vw_ref,
                 *, tq, seq, scale):
    blk = ATTN_BLOCK
    d = ATTN_HEAD_DIM
    kw_ref[0:blk, :] = kp_ref[...]
    kw_ref[blk:blk + tq, :] = k_ref[...]
    kw_ref[blk + tq:, :] = kn_ref[...]
    vw_ref[0:blk, :] = vp_ref[...]
    vw_ref[blk:blk + tq, :] = v_ref[...]
    vw_ref[blk + tq:, :] = vn_ref[...]
    step0 = pl.program_id(1) * tq
    key_off = lax.broadcasted_iota(jnp.int32, (1, 3 * blk), 1) - blk
    n_sub = tq // blk
    for g in range(KV_HEADS):
        for i in range(n_sub):
            kwin = kw_ref[i * blk:(i + 3) * blk, g * d:(g + 1) * d]
            vwin = vw_ref[i * blk:(i + 3) * blk, g * d:(g + 1) * d]
            q_rows = q_ref[g, i * blk:(i + 1) * blk, :]
            q4 = jnp.concatenate([q_rows[:, r * d:(r + 1) * d] for r in range(Q_PER_KV)], axis=0)
            logits = lax.dot_general(q4, kwin, (((1,), (1,)), ((), ())), preferred_element_type=F32)
            logits = logits * (scale * LOG2_E) + bias_ref[g]
            if i == 0 or i == n_sub - 1:
                key_pos = step0 + i * blk + key_off
                logits = jnp.where((key_pos >= 0) & (key_pos < seq), logits, MASK_VALUE)
            probs, inv = [], []
            for r in range(Q_PER_KV):
                lg = logits[r * blk:(r + 1) * blk]
                sink = sink_ref[g * Q_PER_KV + r] * LOG2_E
                m = jnp.maximum(jnp.max(lg, axis=-1, keepdims=True), sink)
                p = jnp.exp2(lg - m)
                inv.append(1.0 / (jnp.sum(p, axis=-1, keepdims=True) + jnp.exp2(sink - m)))
                probs.append(p.astype(BF16))
            out = jnp.dot(jnp.concatenate(probs, axis=0), vwin, preferred_element_type=F32)
            for r in range(Q_PER_KV):
                o_ref[g, i * blk:(i + 1) * blk, r * d:(r + 1) * d] = (
                    out[r * blk:(r + 1) * blk] * inv[r]).astype(o_ref.dtype)


def _attention(proj, bias, sink, bsz, seq, q_block, k_block, v_block):
    t = proj.shape[1]
    blk = ATTN_BLOCK
    tq = _tile(seq, 512)
    nq = seq // tq
    per = tq // blk
    last = t // blk - 1
    width = Q_PER_KV * ATTN_HEAD_DIM
    assert width == COL_BLOCK and KV_HEADS * ATTN_HEAD_DIM == COL_BLOCK
    qb = q_block // KV_HEADS
    assert qb * KV_HEADS == q_block
    main = lambda cb: pl.BlockSpec((None, tq, COL_BLOCK), lambda b, i: (cb, b * nq + i, 0))
    prev = lambda cb: pl.BlockSpec((None, blk, COL_BLOCK), lambda b, i: (cb, jnp.maximum((b * nq + i) * per - 1, 0), 0))
    nxt = lambda cb: pl.BlockSpec((None, blk, COL_BLOCK), lambda b, i: (cb, jnp.minimum((b * nq + i + 1) * per, last), 0))
    kern = functools.partial(_attn_kernel, tq=tq, seq=seq, scale=ATTN_HEAD_DIM ** -0.5)
    est = (2 * KV_HEADS * tq * COL_BLOCK * 2 * 2 + 4 * (tq + 2 * blk) * COL_BLOCK * 2 + 2 * (tq + 2 * blk) * COL_BLOCK * 2
           + 2 * KV_HEADS * Q_PER_KV * blk * 3 * blk * 4 + 8 * 1024 * 1024)
    window = pltpu.VMEM((tq + 2 * blk, COL_BLOCK), BF16)
    return pl.pallas_call(
        kern,
        grid=(bsz, nq),
        in_specs=[
            pl.BlockSpec(memory_space=pltpu.SMEM),
            pl.BlockSpec((KV_HEADS, tq, COL_BLOCK), lambda b, i: (qb, b * nq + i, 0)),
            main(k_block), prev(k_block), nxt(k_block),
            main(v_block), prev(v_block), nxt(v_block),
            pl.BlockSpec((KV_HEADS, Q_PER_KV * blk, 3 * blk), lambda b, i: (0, 0, 0)),
        ],
        out_specs=pl.BlockSpec((KV_HEADS, tq, COL_BLOCK), lambda b, i: (0, b * nq + i, 0)),
        out_shape=jax.ShapeDtypeStruct((KV_HEADS, t, COL_BLOCK), BF16),
        scratch_shapes=[window, window],
        compiler_params=_params(("parallel", "parallel"), est),
        name="attention",
    )(sink, proj, proj, proj, proj, proj, proj, proj, bias)


def _merge_kernel(ys_ref, at_ref, ga_ref, gb_ref, ws_ref, wa_ref, o_ref):
    def branch(lhs_ref, w_ref):
        acc = None
        for c in range(lhs_ref.shape[0]):
            part = jnp.dot(lhs_ref[c], w_ref[c * COL_BLOCK:(c + 1) * COL_BLOCK, :], preferred_element_type=F32)
            acc = part if acc is None else acc + part
        return acc

    ya = branch(ys_ref, ws_ref)
    yb = branch(at_ref, wa_ref)
    merged = _sigmoid(ga_ref[...].astype(F32)) * ya + _sigmoid(gb_ref[...].astype(F32)) * yb
    o_ref[...] = merged.astype(o_ref.dtype)


def _merge(y_ssm, attn, proj, w_ssm, w_attn, gate_block):
    ks, t, _ = y_ssm.shape
    ka = attn.shape[0]
    dm = w_ssm.shape[1]
    nb = dm // COL_BLOCK
    tm = _tile(t, 1024)
    est = 2 * (ks + ka) * tm * COL_BLOCK * 2 + 2 * (ks + ka) * COL_BLOCK * COL_BLOCK * 2 + 6 * tm * COL_BLOCK * 2 + 3 * tm * COL_BLOCK * 4
    return pl.pallas_call(
        _merge_kernel,
        grid=(t // tm, nb),
        in_specs=[
            pl.BlockSpec((ks, tm, COL_BLOCK), lambda i, j: (0, i, 0)),
            pl.BlockSpec((ka, tm, COL_BLOCK), lambda i, j: (0, i, 0)),
            pl.BlockSpec((None, tm, COL_BLOCK), lambda i, j: (gate_block + _serpentine(i, j, nb), i, 0)),
            pl.BlockSpec((None, tm, COL_BLOCK), lambda i, j: (gate_block + nb + _serpentine(i, j, nb), i, 0)),
            pl.BlockSpec((ks * COL_BLOCK, COL_BLOCK), lambda i, j: (0, _serpentine(i, j, nb))),
            pl.BlockSpec((ka * COL_BLOCK, COL_BLOCK), lambda i, j: (0, _serpentine(i, j, nb))),
        ],
        out_specs=pl.BlockSpec((None, tm, COL_BLOCK), lambda i, j: (_serpentine(i, j, nb), i, 0)),
        out_shape=jax.ShapeDtypeStruct((nb, t, COL_BLOCK), BF16),
        compiler_params=_params(("parallel", "arbitrary"), est),
        name="merge",
    )(y_ssm, attn, proj, proj, w_ssm, w_attn)


def _outproj_kernel(m_ref, w_ref, x_ref, o_ref):
    acc = x_ref[...]
    for c in range(m_ref.shape[0]):
        acc = acc + jnp.dot(m_ref[c], w_ref[c * COL_BLOCK:(c + 1) * COL_BLOCK, :], preferred_element_type=F32)
    o_ref[...] = acc


def _out_proj(merged, w_out, x):
    kb, t, _ = merged.shape
    d = w_out.shape[1]
    tm = _tile(t, 512)
    tn = d
    est = 2 * kb * tm * COL_BLOCK * 2 + 2 * kb * COL_BLOCK * tn * 2 + 4 * tm * tn * 4 + tm * tn * 4
    return pl.pallas_call(
        _outproj_kernel,
        grid=(t // tm, d // tn),
        in_specs=[
            pl.BlockSpec((kb, tm, COL_BLOCK), lambda i, j: (0, i, 0)),
            pl.BlockSpec((kb * COL_BLOCK, tn), lambda i, j: (0, j)),
            pl.BlockSpec((tm, tn), lambda i, j: (i, j)),
        ],
        out_specs=pl.BlockSpec((tm, tn), lambda i, j: (i, j)),
        out_shape=jax.ShapeDtypeStruct((t, d), F32),
        compiler_params=_params(("parallel", "arbitrary"), est),
        name="out_proj",
    )(merged, w_out, x)


def _ffn_kernel(h_ref, nw_ref, wg_ref, wu_ref, wo_ref, fw_ref, o_ref, hn_ref, acc_ref, *, final_norm):
    f = pl.program_id(1)

    @pl.when(f == 0)
    def _():
        hn_ref[...] = _rmsnorm(h_ref[...], nw_ref[...]).astype(BF16)
        acc_ref[...] = jnp.zeros_like(acc_ref)

    hn = hn_ref[...]
    tf = wg_ref.shape[1]
    part = tf // FFN_SPLIT
    acc = acc_ref[...]
    for s in range(FFN_SPLIT):
        cols = slice(s * part, (s + 1) * part)
        gate = jnp.dot(hn, wg_ref[:, cols], preferred_element_type=F32)
        up = jnp.dot(hn, wu_ref[:, cols], preferred_element_type=F32)
        act = (_silu(gate) * up).astype(BF16)
        acc = acc + jnp.dot(act, wo_ref[cols, :], preferred_element_type=F32)
    acc_ref[...] = acc

    @pl.when(f == pl.num_programs(1) - 1)
    def _():
        out = h_ref[...] + acc_ref[...]
        o_ref[...] = _rmsnorm(out, fw_ref[...]) if final_norm else out


def _ffn(h, norm_w, w_in, w_out, final_w, final_norm):
    t, d = h.shape
    ff = w_out.shape[0]
    tm = _tile(t, 512)
    tf = _tile(ff, 512)
    nf = ff // tf
    est = 2 * tm * d * 4 * 2 + tm * d * 2 + tm * d * 4 + 2 * 3 * d * tf * 2 + 4 * tm * tf * 4
    return pl.pallas_call(
        functools.partial(_ffn_kernel, final_norm=final_norm),
        grid=(t // tm, nf),
        in_specs=[
            pl.BlockSpec((tm, d), lambda i, f: (i, 0)),
            pl.BlockSpec((1, d), lambda i, f: (0, 0)),
            pl.BlockSpec((d, tf), lambda i, f: (0, _serpentine(i, f, nf))),
            pl.BlockSpec((d, tf), lambda i, f: (0, nf + _serpentine(i, f, nf))),
            pl.BlockSpec((tf, d), lambda i, f: (_serpentine(i, f, nf), 0)),
            pl.BlockSpec((1, d), lambda i, f: (0, 0)),
        ],
        out_specs=pl.BlockSpec((tm, d), lambda i, f: (i, 0)),
        out_shape=jax.ShapeDtypeStruct((t, d), F32),
        scratch_shapes=[pltpu.VMEM((tm, d), BF16), pltpu.VMEM((tm, d), F32)],
        compiler_params=_params(("parallel", "arbitrary"), est),
        name="ffn",
    )(h, norm_w.reshape(1, d), w_in, w_in, w_out, final_w.reshape(1, d))


def _layer_weights(w_in, conv_w, conv_b, dt_bias, a_log, d_skip, ssm_norm_w, d_model):
    d_inner = ssm_norm_w.shape[0]
    n_heads = d_skip.shape[0]
    gn = SSM_GROUPS * STATE_DIM
    conv_dim = d_inner + 2 * gn
    q_dim = d_model
    kv_dim = KV_HEADS * ATTN_HEAD_DIM
    cuts = np.cumsum([d_inner, conv_dim, 2 * n_heads, q_dim, kv_dim, kv_dim, 2 * d_model])
    z_w, xbc_w, dt_w, q_w, k_w, v_w, gate_w = jnp.split(w_in, cuts[:-1], axis=1)
    w_main = jnp.concatenate([z_w, q_w, xbc_w, gate_w, k_w, v_w], axis=1).astype(BF16)
    blocks = {}
    off = 0
    for name, width in (("z", d_inner), ("q", q_dim), ("xbc", conv_dim), ("gate", 2 * d_model), ("k", kv_dim), ("v", kv_dim)):
        blocks[name] = off // COL_BLOCK
        off += width
    n_conv = conv_dim // COL_BLOCK
    conv_w3 = conv_w.reshape(CONV_WIDTH, n_conv, COL_BLOCK).transpose(1, 0, 2)
    conv_b3 = conv_b.reshape(n_conv, 1, COL_BLOCK)
    dtb = dt_bias.reshape(1, 2 * n_heads)
    a_neg = (-jnp.exp(a_log.astype(F32))).reshape(1, 2 * n_heads)
    dskip = jnp.repeat(d_skip.astype(F32), HEAD_DIM_SSM).reshape(SSM_GROUPS, 1, COL_BLOCK)
    nw = ssm_norm_w.astype(F32).reshape(SSM_GROUPS, 1, COL_BLOCK)
    return w_main, dt_w.astype(BF16), blocks, conv_w3, conv_b3, dtb, a_neg, dskip, nw


def _trunk(x, mix_norm_w, w_in, conv_w, conv_b, dt_bias, a_log, d_skip, ssm_norm_w, w_ssm_branch, rel_bias,
           attn_sink, w_attn_branch, w_out, ffn_norm_w, w_ffn_in, w_ffn_out, final_norm_w):
    bsz, seq, d_model = x.shape
    depth = w_in.shape[0]
    h = x.reshape(bsz * seq, d_model)
    bias = _attention_bias(rel_bias).reshape(KV_HEADS, Q_PER_KV * ATTN_BLOCK, 3 * ATTN_BLOCK)
    for l in range(depth):
        w_main, w_dt, blocks, conv_w3, conv_b3, dtb, a_neg, dskip, nw = _layer_weights(
            w_in[l], conv_w[l], conv_b[l], dt_bias[l], a_log[l], d_skip[l], ssm_norm_w[l], d_model)
        proj, dt_raw = _in_proj(h, mix_norm_w[l], w_main, w_dt)
        n_xs = ssm_norm_w.shape[1] // COL_BLOCK
        xs, bm, cm = _conv_silu(proj, conv_w3, conv_b3, bsz, seq, blocks["xbc"], n_xs)
        y_fwd = _ssd_forward(xs, bm, cm, dt_raw, dtb, a_neg, dskip, bsz, seq)
        y_ssm = _ssd_backward(xs, bm, cm, dt_raw, dtb, a_neg, y_fwd, proj, nw, bsz, seq, blocks["z"])
        attn = _attention(proj, bias, attn_sink[l], bsz, seq, blocks["q"], blocks["k"], blocks["v"])
        merged = _merge(y_ssm, attn, proj, w_ssm_branch[l].astype(BF16), w_attn_branch[l].astype(BF16), blocks["gate"])
        h = _out_proj(merged, w_out[l].astype(BF16), h)
        h = _ffn(h, ffn_norm_w[l], w_ffn_in[l].astype(BF16), w_ffn_out[l].astype(BF16), final_norm_w,
                 final_norm=l == depth - 1)
    return h.reshape(bsz, seq, d_model)


def kernel(x_prompt, x_sample, mix_norm_w, w_in, conv_w, conv_b, dt_bias, a_log, d_skip, ssm_norm_w, w_ssm_branch,
           rel_bias, attn_sink, w_attn_branch, w_out, ffn_norm_w, w_ffn_in, w_ffn_out, final_norm_w):
    args = (mix_norm_w, w_in, conv_w, conv_b, dt_bias, a_log, d_skip, ssm_norm_w, w_ssm_branch, rel_bias, attn_sink,
            w_attn_branch, w_out, ffn_norm_w, w_ffn_in, w_ffn_out, final_norm_w)
    return (_trunk(x_prompt, *args), _trunk(x_sample, *args))
```

```python
import functools
import math

import numpy as np
import jax
import jax.numpy as jnp
from jax import lax
from jax.experimental import pallas as pl
from jax.experimental.pallas import tpu as pltpu

F32 = jnp.float32
BF16 = jnp.bfloat16

EPS = 1e-6
HEAD_DIM_SSM = 64
STATE_DIM = 128
SSM_GROUPS = 8
CHUNK = 128
FFN_SPLIT = 2
SSD_STEP_CHUNKS = 4
CONV_WIDTH = 5
ATTN_HEAD_DIM = 128
KV_HEADS = 4
Q_PER_KV = 4
WINDOW = 128
ATTN_BLOCK = 128
REL_BUCKETS = 32
REL_MAX_DIST = 128
COL_BLOCK = 512
SUBLANES = 8
HALO_ROWS = 16
CONV_ROWS = 128
CONV_PAD = 64
CONV_WINDOWS = 2
LOG2_E = math.log2(math.e)
MASK_VALUE = -1e30
VMEM_CAP = 60000 * 1024


def _vmem_limit(estimate_bytes):
    return int(min(VMEM_CAP, max(estimate_bytes * 5 // 4, 16 * 1024 * 1024)))


def _params(semantics, vmem_estimate):
    return pltpu.CompilerParams(dimension_semantics=semantics,
                                vmem_limit_bytes=_vmem_limit(vmem_estimate))


def _sigmoid(x):
    return 1.0 / (1.0 + jnp.exp2(x * -LOG2_E))


def _silu(x):
    return x * _sigmoid(x)


def _softplus(x):
    return jnp.maximum(x, 0.0) + jnp.log1p(jnp.exp(-jnp.abs(x)))


def _rmsnorm(x, w):
    ms = jnp.mean(x * x, axis=-1, keepdims=True)
    return x * lax.rsqrt(ms + EPS) * w


def _tile(n, want):
    t = min(n, want)
    while n % t:
        t //= 2
    return t


def _serpentine(i, j, n):
    return jnp.where(i % 2 == 0, j, n - 1 - j)


def _inproj_kernel(x_ref, nw_ref, w_ref, wdt_ref, o_ref, dt_ref, xn_ref):
    @pl.when(pl.program_id(1) == 0)
    def _():
        xn = _rmsnorm(x_ref[...], nw_ref[...]).astype(BF16)
        xn_ref[...] = xn
        dt_ref[...] = jnp.dot(xn, wdt_ref[...], preferred_element_type=F32)

    out = jnp.dot(xn_ref[...], w_ref[...], preferred_element_type=F32).astype(o_ref.dtype)
    for c in range(o_ref.shape[0]):
        o_ref[c] = out[:, c * COL_BLOCK:(c + 1) * COL_BLOCK]


def _in_proj(x, norm_w, w_main, w_dt):
    t, d = x.shape
    n = w_main.shape[1]
    nb = n // COL_BLOCK
    tm = _tile(t, 1024)
    per = 2 if nb % 2 == 0 else 1
    tn = per * COL_BLOCK
    ndt = w_dt.shape[1]
    nj = nb // per
    est = 2 * tm * d * 4 + tm * d * 2 + 2 * d * tn * 2 + 2 * tm * tn * 2 + tm * tn * 4 + 2 * d * ndt * 2 + 2 * tm * ndt * 4
    return pl.pallas_call(
        _inproj_kernel,
        grid=(t // tm, nj),
        in_specs=[
            pl.BlockSpec((tm, d), lambda i, j: (i, 0)),
            pl.BlockSpec((1, d), lambda i, j: (0, 0)),
            pl.BlockSpec((d, tn), lambda i, j: (0, _serpentine(i, j, nj))),
            pl.BlockSpec((d, ndt), lambda i, j: (0, 0)),
        ],
        out_specs=[
            pl.BlockSpec((per, tm, COL_BLOCK), lambda i, j: (_serpentine(i, j, nj), i, 0)),
            pl.BlockSpec((tm, ndt), lambda i, j: (i, 0)),
        ],
        out_shape=[
            jax.ShapeDtypeStruct((nb, t, COL_BLOCK), BF16),
            jax.ShapeDtypeStruct((t, ndt), F32),
        ],
        scratch_shapes=[pltpu.VMEM((tm, d), BF16)],
        compiler_params=_params(("parallel", "arbitrary"), est),
        name="in_proj",
    )(x, norm_w.reshape(1, d), w_main, w_dt)


def _shift_matrix():
    half = CONV_WIDTH // 2
    n_shift = CONV_WIDTH - 1
    rows = np.arange(n_shift * CONV_ROWS)[:, None]
    cols = np.arange(CONV_ROWS + 2 * CONV_PAD)[None, :]
    q, rem = rows // (n_shift * SUBLANES), rows % (n_shift * SUBLANES)
    blk, t = rem // SUBLANES, q * SUBLANES + rem % SUBLANES
    off = blk - half + (blk >= half)
    return jnp.asarray(cols == t + CONV_PAD + off, BF16)


def _conv_kernel(main_ref, prev_ref, next_ref, w_ref, b_ref, shift_ref, xs_ref, bm_ref, cm_ref, win_ref, *, tr, n_xs):
    i = pl.program_id(1)
    last = pl.num_programs(1) - 1
    half = CONV_WIDTH // 2
    n_blocks = main_ref.shape[0]
    shift = shift_ref[...]
    for slot in range(win_ref.shape[0]):
        win_ref[slot, 0:CONV_PAD - HALO_ROWS, :] = jnp.zeros((CONV_PAD - HALO_ROWS, COL_BLOCK), BF16)
        win_ref[slot, CONV_PAD + tr + HALO_ROWS:, :] = jnp.zeros((CONV_PAD - HALO_ROWS, COL_BLOCK), BF16)

    def conv_block(c, slot):
        halo_zero = jnp.zeros((HALO_ROWS, COL_BLOCK), BF16)
        win_ref[slot, CONV_PAD - HALO_ROWS:CONV_PAD, :] = jnp.where(i > 0, prev_ref[c], halo_zero)
        win_ref[slot, CONV_PAD:CONV_PAD + tr, :] = main_ref[c]
        win_ref[slot, CONV_PAD + tr:CONV_PAD + tr + HALO_ROWS, :] = jnp.where(i < last, next_ref[c], halo_zero)
        w = w_ref[c]
        bias = b_ref[c]
        outs = []
        for r in range(tr // CONV_ROWS):
            xwin = win_ref[slot, r * CONV_ROWS:(r + 1) * CONV_ROWS + 2 * CONV_PAD, :]
            taps = jnp.dot(shift, xwin, preferred_element_type=F32)
            centre = xwin[CONV_PAD:CONV_PAD + CONV_ROWS].astype(F32)
            groups = []
            for q in range(CONV_ROWS // SUBLANES):
                acc = bias + centre[q * SUBLANES:(q + 1) * SUBLANES] * w[half:half + 1]
                for b in range(CONV_WIDTH - 1):
                    j = b + (b >= half)
                    row0 = (q * (CONV_WIDTH - 1) + b) * SUBLANES
                    acc = acc + taps[row0:row0 + SUBLANES] * w[j:j + 1]
                groups.append(acc)
            outs.append(_silu(jnp.concatenate(groups, axis=0)))
        return outs

    def xs_body(k, carry):
        for slot in range(CONV_WINDOWS):
            c = k * CONV_WINDOWS + slot
            for r, act in enumerate(conv_block(c, slot)):
                xs_ref[c, r * CONV_ROWS:(r + 1) * CONV_ROWS, :] = act.astype(xs_ref.dtype)
        return carry

    lax.fori_loop(0, n_xs // CONV_WINDOWS, xs_body, 0)
    per = COL_BLOCK // STATE_DIM
    n_b = (n_blocks - n_xs) // 2

    for first, dst in ((n_xs, bm_ref), (n_xs + n_b, cm_ref)):
        for c in range(first, first + n_b):
            for r, act in enumerate(conv_block(c, c - n_xs)):
                act = act.astype(dst.dtype)
                for lane_blk in range(per):
                    dst[(c - first) * per + lane_blk, r * CONV_ROWS:(r + 1) * CONV_ROWS, :] = (
                        act[:, lane_blk * STATE_DIM:(lane_blk + 1) * STATE_DIM])


def _conv_silu(proj, conv_w, conv_b, bsz, seq, first_block, n_xs):
    t = proj.shape[1]
    n_blocks = conv_w.shape[0]
    tr = _tile(seq, 512)
    nr = seq // tr
    hb = tr // HALO_ROWS
    last_halo = t // HALO_ROWS - 1
    n_bc = (n_blocks - n_xs) // 2 * (COL_BLOCK // STATE_DIM)
    cb0 = first_block // n_blocks
    assert cb0 * n_blocks == first_block
    assert tr % CONV_ROWS == 0
    est = (2 * n_blocks * (tr + 2 * HALO_ROWS) * COL_BLOCK * 2 + 2 * (n_xs * tr * COL_BLOCK + 2 * n_bc * tr * STATE_DIM) * 2
           + max(CONV_WINDOWS, n_blocks - n_xs) * (tr + 2 * CONV_PAD) * COL_BLOCK * 2 + 4 * n_blocks * 8 * COL_BLOCK * 4
           + 8 * 1024 * 1024)
    shift = _shift_matrix()
    kern = functools.partial(_conv_kernel, tr=tr, n_xs=n_xs)
    return pl.pallas_call(
        kern,
        grid=(bsz, nr),
        in_specs=[
            pl.BlockSpec((n_blocks, tr, COL_BLOCK), lambda b, i: (cb0, b * nr + i, 0)),
            pl.BlockSpec((n_blocks, HALO_ROWS, COL_BLOCK),
                         lambda b, i: (cb0, jnp.maximum((b * nr + i) * hb - 1, 0), 0)),
            pl.BlockSpec((n_blocks, HALO_ROWS, COL_BLOCK),
                         lambda b, i: (cb0, jnp.minimum((b * nr + i + 1) * hb, last_halo), 0)),
            pl.BlockSpec((n_blocks, CONV_WIDTH, COL_BLOCK), lambda b, i: (0, 0, 0)),
            pl.BlockSpec((n_blocks, 1, COL_BLOCK), lambda b, i: (0, 0, 0)),
            pl.BlockSpec(shift.shape, lambda b, i: (0, 0)),
        ],
        out_specs=[
            pl.BlockSpec((n_xs, tr, COL_BLOCK), lambda b, i: (0, b * nr + i, 0)),
            pl.BlockSpec((n_bc, tr, STATE_DIM), lambda b, i: (0, b * nr + i, 0)),
            pl.BlockSpec((n_bc, tr, STATE_DIM), lambda b, i: (0, b * nr + i, 0)),
        ],
        out_shape=[
            jax.ShapeDtypeStruct((n_xs, t, COL_BLOCK), BF16),
            jax.ShapeDtypeStruct((n_bc, t, STATE_DIM), BF16),
            jax.ShapeDtypeStruct((n_bc, t, STATE_DIM), BF16),
        ],
        scratch_shapes=[pltpu.VMEM((max(CONV_WINDOWS, n_blocks - n_xs), tr + 2 * CONV_PAD, COL_BLOCK), BF16)],
        compiler_params=_params(("parallel", "parallel"), est),
        name="conv_silu",
    )(proj, proj, proj, conv_w, conv_b, shift)


def _scan_rows(v, reverse):
    n = v.shape[0]
    row = lax.broadcasted_iota(jnp.int32, v.shape, 0)
    k = 1
    while k < n:
        if reverse:
            shifted = pltpu.roll(v, n - k, axis=0)
            keep = row < n - k
        else:
            shifted = pltpu.roll(v, k, axis=0)
            keep = row >= k
        v = v + jnp.where(keep, shifted, 0.0)
        k *= 2
    return v


def _ssd_chunk_prologue(dt_raw, dtb_ref, a_ref, a_sc, at_sc, wdt_sc, reverse):
    dt = _softplus(dt_raw + dtb_ref[...])
    acc = _scan_rows(dt * a_ref[...], reverse)
    a_sc[...] = acc * LOG2_E
    acc_t = acc.T
    dt_t = dt.T
    at_sc[...] = acc_t * LOG2_E - jnp.log2(dt_t)
    n = acc.shape[0]
    total = acc_t[:, 0:1] if reverse else acc_t[:, n - 1:n]
    wdt_sc[...] = jnp.exp(total - acc_t) * dt_t


def _ssd_group(g, rows, xs_ref, bm_ref, cm_ref, st_ref, a_sc, at_sc, wdt_sc, head0, reverse):
    l = CHUNK
    xs = xs_ref[g, rows, :]
    bm = bm_ref[g, rows, :]
    cm = cm_ref[g, rows, :]
    cb = lax.dot_general(cm, bm, (((1,), (1,)), ((), ())), preferred_element_type=F32).astype(BF16)
    bm_t = bm.astype(F32).T.astype(BF16)

    heads_per_group = COL_BLOCK // HEAD_DIM_SSM
    first = head0 + g * heads_per_group
    a_all = a_sc[...]
    row_i = lax.broadcasted_iota(jnp.int32, (l, l), 0)
    col_i = lax.broadcasted_iota(jnp.int32, (l, l), 1)
    causal = (row_i <= col_i) if reverse else (row_i >= col_i)
    lane = lax.broadcasted_iota(jnp.int32, (l, 2 * HEAD_DIM_SSM), 1)
    low_half = lane < HEAD_DIM_SSM
    keep_low = jnp.where(low_half, 1.0, 0.0).astype(BF16)
    keep_high = jnp.where(low_half, 0.0, 1.0).astype(BF16)
    edge = 0 if reverse else l - 1

    y_slabs = []
    for pair in range(heads_per_group // 2):
        sl = slice(pair * 2 * HEAD_DIM_SSM, (pair + 1) * 2 * HEAD_DIM_SSM)
        st_pair = st_ref[g, :, sl]
        y_off = jnp.dot(cm, st_pair.astype(BF16), preferred_element_type=F32)
        xs_pair = xs[:, sl]
        rhs = jnp.concatenate([xs_pair * keep_low, xs_pair * keep_high], axis=0)
        m_parts, bw_parts, a_col_parts, a_tot_parts = [], [], [], []
        for sub in range(2):
            h = first + pair * 2 + sub
            a_col = a_all[:, h:h + 1]
            a_row = at_sc[h:h + 1, :]
            wd_row = wdt_sc[h:h + 1, :]
            seg = jnp.where(causal, a_col - a_row, -jnp.inf)
            m_parts.append(cb * jnp.exp2(seg).astype(BF16))
            bw_parts.append(bm_t * wd_row.astype(BF16))
            a_col_parts.append(a_col)
            a_tot_parts.append(a_all[edge:edge + 1, h:h + 1])
        m_pair = jnp.concatenate(m_parts, axis=1)
        bw_pair = jnp.concatenate(bw_parts, axis=1)
        a_col_pair = jnp.where(low_half, a_col_parts[0], a_col_parts[1])
        a_tot_pair = jnp.where(low_half[0:1], a_tot_parts[0], a_tot_parts[1])
        y_slabs.append(jnp.dot(m_pair, rhs, preferred_element_type=F32) + jnp.exp2(a_col_pair) * y_off)
        st_ref[g, :, sl] = jnp.exp2(a_tot_pair) * st_pair + jnp.dot(bw_pair, rhs, preferred_element_type=F32)
    return jnp.concatenate(y_slabs, axis=1), xs


def _chunk_rows(index):
    return pl.ds(pl.multiple_of(index * CHUNK, CHUNK), CHUNK)


def _ssd_fwd_kernel(xs_ref, bm_ref, cm_ref, dt_ref, dtb_ref, a_ref, dskip_ref, y_ref, st_ref, a_sc, at_sc, wdt_sc):
    @pl.when(pl.program_id(1) == 0)
    def _():
        st_ref[...] = jnp.zeros_like(st_ref)

    def chunk(c, carry):
        rows = _chunk_rows(c)
        _ssd_chunk_prologue(dt_ref[rows, :], dtb_ref, a_ref, a_sc, at_sc, wdt_sc, reverse=False)
        for g in range(SSM_GROUPS):
            y, xs = _ssd_group(g, rows, xs_ref, bm_ref, cm_ref, st_ref, a_sc, at_sc, wdt_sc, head0=0, reverse=False)
            y_ref[g, rows, :] = (y + xs.astype(F32) * dskip_ref[g]).astype(y_ref.dtype)
        return carry

    lax.fori_loop(0, dt_ref.shape[0] // CHUNK, chunk, 0)


def _ssd_bwd_kernel(xs_ref, bm_ref, cm_ref, dt_ref, dtb_ref, a_ref, yf_ref, z_ref, nw_ref, y_ref,
                    st_ref, a_sc, at_sc, wdt_sc, *, n_heads):
    @pl.when(pl.program_id(1) == 0)
    def _():
        st_ref[...] = jnp.zeros_like(st_ref)

    n_chunks = dt_ref.shape[0] // CHUNK

    def chunk(it, carry):
        rows = _chunk_rows(n_chunks - 1 - it)
        _ssd_chunk_prologue(dt_ref[rows, :], dtb_ref, a_ref, a_sc, at_sc, wdt_sc, reverse=True)
        for g in range(SSM_GROUPS):
            y, _ = _ssd_group(g, rows, xs_ref, bm_ref, cm_ref, st_ref, a_sc, at_sc, wdt_sc, head0=n_heads,
                              reverse=True)
            y = y + yf_ref[g, rows, :].astype(F32)
            y = y * _silu(z_ref[g, rows, :].astype(F32))
            y_ref[g, rows, :] = _rmsnorm(y, nw_ref[g]).astype(y_ref.dtype)
        return carry

    lax.fori_loop(0, n_chunks, chunk, 0)


def _ssd_scratch(two_h):
    return [
        pltpu.VMEM((SSM_GROUPS, STATE_DIM, COL_BLOCK), F32),
        pltpu.VMEM((CHUNK, two_h), F32),
        pltpu.VMEM((two_h, CHUNK), F32),
        pltpu.VMEM((two_h, CHUNK), F32),
    ]


def _ssd_est(two_h, n_slabs, rows):
    return (2 * n_slabs * SSM_GROUPS * rows * COL_BLOCK * 2 + 4 * SSM_GROUPS * rows * STATE_DIM * 2
            + SSM_GROUPS * STATE_DIM * COL_BLOCK * 4 + 2 * rows * two_h * 4 + 6 * CHUNK * two_h * 4 + 8 * 1024 * 1024)


def _ssd_forward(xs, bm, cm, dt_raw, dt_bias, a_neg, d_skip, bsz, seq):
    g, t, _ = xs.shape
    rows = _tile(seq, SSD_STEP_CHUNKS * CHUNK)
    nc = seq // rows
    two_h = dt_raw.shape[1]
    slab = pl.BlockSpec((g, rows, COL_BLOCK), lambda b, c: (0, b * nc + c, 0))
    bc = pl.BlockSpec((g, rows, STATE_DIM), lambda b, c: (0, b * nc + c, 0))
    row = pl.BlockSpec((1, two_h), lambda b, c: (0, 0))
    par = pl.BlockSpec((g, 1, COL_BLOCK), lambda b, c: (0, 0, 0))
    return pl.pallas_call(
        _ssd_fwd_kernel,
        grid=(bsz, nc),
        in_specs=[slab, bc, bc, pl.BlockSpec((rows, two_h), lambda b, c: (b * nc + c, 0)), row, row, par],
        out_specs=slab,
        out_shape=jax.ShapeDtypeStruct((g, t, COL_BLOCK), BF16),
        scratch_shapes=_ssd_scratch(two_h),
        compiler_params=_params(("parallel", "arbitrary"), _ssd_est(two_h, 2, rows)),
        name="ssd_fwd",
    )(xs, bm, cm, dt_raw, dt_bias, a_neg, d_skip)


def _ssd_backward(xs, bm, cm, dt_raw, dt_bias, a_neg, y_fwd, proj, norm_w, bsz, seq, z_block):
    g, t, _ = xs.shape
    rows = _tile(seq, SSD_STEP_CHUNKS * CHUNK)
    nc = seq // rows
    two_h = dt_raw.shape[1]
    rev = lambda b, c: (0, b * nc + nc - 1 - c, 0)
    slab = pl.BlockSpec((g, rows, COL_BLOCK), rev)
    bc = pl.BlockSpec((g, rows, STATE_DIM), rev)
    row = pl.BlockSpec((1, two_h), lambda b, c: (0, 0))
    par = pl.BlockSpec((g, 1, COL_BLOCK), lambda b, c: (0, 0, 0))
    zb = z_block // g
    assert zb * g == z_block
    kern = functools.partial(_ssd_bwd_kernel, n_heads=two_h // 2)
    return pl.pallas_call(
        kern,
        grid=(bsz, nc),
        in_specs=[slab, bc, bc, pl.BlockSpec((rows, two_h), lambda b, c: (b * nc + nc - 1 - c, 0)), row, row,
                  slab, pl.BlockSpec((g, rows, COL_BLOCK), lambda b, c: (zb, b * nc + nc - 1 - c, 0)), par],
        out_specs=slab,
        out_shape=jax.ShapeDtypeStruct((g, t, COL_BLOCK), BF16),
        scratch_shapes=_ssd_scratch(two_h),
        compiler_params=_params(("parallel", "arbitrary"), _ssd_est(two_h, 4, rows)),
        name="ssd_bwd",
    )(xs, bm, cm, dt_raw, dt_bias, a_neg, y_fwd, proj, norm_w)


def _t5_buckets(rel):
    half = REL_BUCKETS // 2
    ret = (rel > 0).astype(np.int32) * half
    n = np.abs(rel)
    max_exact = half // 2
    large = max_exact + (np.log(np.maximum(n, 1) / max_exact) / np.log(REL_MAX_DIST / max_exact)
                         * (half - max_exact)).astype(np.int32)
    large = np.minimum(large, half - 1)
    return ret + np.where(n < max_exact, n, large).astype(np.int32)


def _bias_kernel(rel_bias_ref, bucket_ref, o_ref):
    h = pl.program_id(0)
    bucket = bucket_ref[...]
    acc = jnp.zeros(bucket.shape, F32)
    for b in range(REL_BUCKETS):
        acc = jnp.where(bucket == b, rel_bias_ref[b, h], acc)
    o_ref[...] = jnp.where(bucket < 0, MASK_VALUE, acc * LOG2_E)


def _attention_bias(rel_bias):
    blk = ATTN_BLOCK
    rel = np.arange(3 * blk)[None, :] - blk - np.arange(blk)[:, None]
    bucket = np.where(np.abs(rel) <= WINDOW, _t5_buckets(rel), -1).astype(np.int32)
    heads = rel_bias.shape[1]
    return pl.pallas_call(
        _bias_kernel,
        grid=(heads,),
        in_specs=[pl.BlockSpec(memory_space=pltpu.SMEM), pl.BlockSpec((blk, 3 * blk), lambda h: (0, 0))],
        out_specs=pl.BlockSpec((None, blk, 3 * blk), lambda h: (h, 0, 0)),
        out_shape=jax.ShapeDtypeStruct((heads, blk, 3 * blk), F32),
        compiler_params=_params(("arbitrary",), 4 * 1024 * 1024),
        name="attn_bias",
    )(rel_bias, jnp.asarray(bucket))


def _attn_kernel(sink_ref, q_ref, k_ref, kp_ref, kn_ref, v_ref, vp_ref, vn_ref, bias_ref, o_ref, kw_ref, vw_ref,
                 *, tq, seq, scale):
    blk = ATTN_BLOCK
    d = ATTN_HEAD_DIM
    kw_ref[0:blk, :] = kp_ref[...]
    kw_ref[blk:blk + tq, :] = k_ref[...]
    kw_ref[blk + tq:, :] = kn_ref[...]
    vw_ref[0:blk, :] = vp_ref[...]
    vw_ref[blk:blk + tq, :] = v_ref[...]
    vw_ref[blk + tq:, :] = vn_ref[...]
    step0 = pl.program_id(1) * tq
    key_off = lax.broadcasted_iota(jnp.int32, (1, 3 * blk), 1) - blk
    n_sub = tq // blk
    for g in range(KV_HEADS):
        for i in range(n_sub):
            kwin = kw_ref[i * blk:(i + 3) * blk, g * d:(g + 1) * d]
            vwin = vw_ref[i * blk:(i + 3) * blk, g * d:(g + 1) * d]
            q_rows = q_ref[g, i * blk:(i + 1) * blk, :]
            q4 = jnp.concatenate([q_rows[:, r * d:(r + 1) * d] for r in range(Q_PER_KV)], axis=0)
            logits = lax.dot_general(q4, kwin, (((1,), (1,)), ((), ())), preferred_element_type=F32)
            logits = logits * (scale * LOG2_E) + bias_ref[g]
            if i == 0 or i == n_sub - 1:
                key_pos = step0 + i * blk + key_off
                logits = jnp.where((key_pos >= 0) & (key_pos < seq), logits, MASK_VALUE)
            probs, inv = [], []
            for r in range(Q_PER_KV):
                lg = logits[r * blk:(r + 1) * blk]
                sink = sink_ref[g * Q_PER_KV + r] * LOG2_E
                m = jnp.maximum(jnp.max(lg, axis=-1, keepdims=True), sink)
                p = jnp.exp2(lg - m)
                inv.append(1.0 / (jnp.sum(p, axis=-1, keepdims=True) + jnp.exp2(sink - m)))
                probs.append(p.astype(BF16))
            out = jnp.dot(jnp.concatenate(probs, axis=0), vwin, preferred_element_type=F32)
            for r in range(Q_PER_KV):
                o_ref[g, i * blk:(i + 1) * blk, r * d:(r + 1) * d] = (
                    out[r * blk:(r + 1) * blk] * inv[r]).astype(o_ref.dtype)


def _attention(proj, bias, sink, bsz, seq, q_block, k_block, v_block):
    t = proj.shape[1]
    blk = ATTN_BLOCK
    tq = _tile(seq, 512)
    nq = seq // tq
    per = tq // blk
    last = t // blk - 1
    width = Q_PER_KV * ATTN_HEAD_DIM
    assert width == COL_BLOCK and KV_HEADS * ATTN_HEAD_DIM == COL_BLOCK
    qb = q_block // KV_HEADS
    assert qb * KV_HEADS == q_block
    main = lambda cb: pl.BlockSpec((None, tq, COL_BLOCK), lambda b, i: (cb, b * nq + i, 0))
    prev = lambda cb: pl.BlockSpec((None, blk, COL_BLOCK), lambda b, i: (cb, jnp.maximum((b * nq + i) * per - 1, 0), 0))
    nxt = lambda cb: pl.BlockSpec((None, blk, COL_BLOCK), lambda b, i: (cb, jnp.minimum((b * nq + i + 1) * per, last), 0))
    kern = functools.partial(_attn_kernel, tq=tq, seq=seq, scale=ATTN_HEAD_DIM ** -0.5)
    est = (2 * KV_HEADS * tq * COL_BLOCK * 2 * 2 + 4 * (tq + 2 * blk) * COL_BLOCK * 2 + 2 * (tq + 2 * blk) * COL_BLOCK * 2
           + 2 * KV_HEADS * Q_PER_KV * blk * 3 * blk * 4 + 8 * 1024 * 1024)
    window = pltpu.VMEM((tq + 2 * blk, COL_BLOCK), BF16)
    return pl.pallas_call(
        kern,
        grid=(bsz, nq),
        in_specs=[
            pl.BlockSpec(memory_space=pltpu.SMEM),
            pl.BlockSpec((KV_HEADS, tq, COL_BLOCK), lambda b, i: (qb, b * nq + i, 0)),
            main(k_block), prev(k_block), nxt(k_block),
            main(v_block), prev(v_block), nxt(v_block),
            pl.BlockSpec((KV_HEADS, Q_PER_KV * blk, 3 * blk), lambda b, i: (0, 0, 0)),
        ],
        out_specs=pl.BlockSpec((KV_HEADS, tq, COL_BLOCK), lambda b, i: (0, b * nq + i, 0)),
        out_shape=jax.ShapeDtypeStruct((KV_HEADS, t, COL_BLOCK), BF16),
        scratch_shapes=[window, window],
        compiler_params=_params(("parallel", "parallel"), est),
        name="attention",
    )(sink, proj, proj, proj, proj, proj, proj, proj, bias)


def _merge_kernel(ys_ref, at_ref, ga_ref, gb_ref, ws_ref, wa_ref, o_ref):
    def branch(lhs_ref, w_ref):
        acc = None
        for c in range(lhs_ref.shape[0]):
            part = jnp.dot(lhs_ref[c], w_ref[c * COL_BLOCK:(c + 1) * COL_BLOCK, :], preferred_element_type=F32)
            acc = part if acc is None else acc + part
        return acc

    ya = branch(ys_ref, ws_ref)
    yb = branch(at_ref, wa_ref)
    merged = _sigmoid(ga_ref[...].astype(F32)) * ya + _sigmoid(gb_ref[...].astype(F32)) * yb
    o_ref[...] = merged.astype(o_ref.dtype)


def _merge(y_ssm, attn, proj, w_ssm, w_attn, gate_block):
    ks, t, _ = y_ssm.shape
    ka = attn.shape[0]
    dm = w_ssm.shape[1]
    nb = dm // COL_BLOCK
    tm = _tile(t, 1024)
    est = 2 * (ks + ka) * tm * COL_BLOCK * 2 + 2 * (ks + ka) * COL_BLOCK * COL_BLOCK * 2 + 6 * tm * COL_BLOCK * 2 + 3 * tm * COL_BLOCK * 4
    return pl.pallas_call(
        _merge_kernel,
        grid=(t // tm, nb),
        in_specs=[
            pl.BlockSpec((ks, tm, COL_BLOCK), lambda i, j: (0, i, 0)),
            pl.BlockSpec((ka, tm, COL_BLOCK), lambda i, j: (0, i, 0)),
            pl.BlockSpec((None, tm, COL_BLOCK), lambda i, j: (gate_block + _serpentine(i, j, nb), i, 0)),
            pl.BlockSpec((None, tm, COL_BLOCK), lambda i, j: (gate_block + nb + _serpentine(i, j, nb), i, 0)),
            pl.BlockSpec((ks * COL_BLOCK, COL_BLOCK), lambda i, j: (0, _serpentine(i, j, nb))),
            pl.BlockSpec((ka * COL_BLOCK, COL_BLOCK), lambda i, j: (0, _serpentine(i, j, nb))),
        ],
        out_specs=pl.BlockSpec((None, tm, COL_BLOCK), lambda i, j: (_serpentine(i, j, nb), i, 0)),
        out_shape=jax.ShapeDtypeStruct((nb, t, COL_BLOCK), BF16),
        compiler_params=_params(("parallel", "arbitrary"), est),
        name="merge",
    )(y_ssm, attn, proj, proj, w_ssm, w_attn)


def _outproj_kernel(m_ref, w_ref, x_ref, o_ref):
    acc = x_ref[...]
    for c in range(m_ref.shape[0]):
        acc = acc + jnp.dot(m_ref[c], w_ref[c * COL_BLOCK:(c + 1) * COL_BLOCK, :], preferred_element_type=F32)
    o_ref[...] = acc


def _out_proj(merged, w_out, x):
    kb, t, _ = merged.shape
    d = w_out.shape[1]
    tm = _tile(t, 512)
    tn = d
    est = 2 * kb * tm * COL_BLOCK * 2 + 2 * kb * COL_BLOCK * tn * 2 + 4 * tm * tn * 4 + tm * tn * 4
    return pl.pallas_call(
        _outproj_kernel,
        grid=(t // tm, d // tn),
        in_specs=[
            pl.BlockSpec((kb, tm, COL_BLOCK), lambda i, j: (0, i, 0)),
            pl.BlockSpec((kb * COL_BLOCK, tn), lambda i, j: (0, j)),
            pl.BlockSpec((tm, tn), lambda i, j: (i, j)),
        ],
        out_specs=pl.BlockSpec((tm, tn), lambda i, j: (i, j)),
        out_shape=jax.ShapeDtypeStruct((t, d), F32),
        compiler_params=_params(("parallel", "arbitrary"), est),
        name="out_proj",
    )(merged, w_out, x)


def _ffn_kernel(h_ref, nw_ref, wg_ref, wu_ref, wo_ref, fw_ref, o_ref, hn_ref, acc_ref, *, final_norm):
    f = pl.program_id(1)

    @pl.when(f == 0)
    def _():
        hn_ref[...] = _rmsnorm(h_ref[...], nw_ref[...]).astype(BF16)
        acc_ref[...] = jnp.zeros_like(acc_ref)

    hn = hn_ref[...]
    tf = wg_ref.shape[1]
    part = tf // FFN_SPLIT
    acc = acc_ref[...]
    for s in range(FFN_SPLIT):
        cols = slice(s * part, (s + 1) * part)
        gate = jnp.dot(hn, wg_ref[:, cols], preferred_element_type=F32)
        up = jnp.dot(hn, wu_ref[:, cols], preferred_element_type=F32)
        act = (_silu(gate) * up).astype(BF16)
        acc = acc + jnp.dot(act, wo_ref[cols, :], preferred_element_type=F32)
    acc_ref[...] = acc

    @pl.when(f == pl.num_programs(1) - 1)
    def _():
        out = h_ref[...] + acc_ref[...]
        o_ref[...] = _rmsnorm(out, fw_ref[...]) if final_norm else out


def _ffn(h, norm_w, w_in, w_out, final_w, final_norm):
    t, d = h.shape
    ff = w_out.shape[0]
    tm = _tile(t, 512)
    tf = _tile(ff, 512)
    nf = ff // tf
    est = 2 * tm * d * 4 * 2 + tm * d * 2 + tm * d * 4 + 2 * 3 * d * tf * 2 + 4 * tm * tf * 4
    return pl.pallas_call(
        functools.partial(_ffn_kernel, final_norm=final_norm),
        grid=(t // tm, nf),
        in_specs=[
            pl.BlockSpec((tm, d), lambda i, f: (i, 0)),
            pl.BlockSpec((1, d), lambda i, f: (0, 0)),
            pl.BlockSpec((d, tf), lambda i, f: (0, _serpentine(i, f, nf))),
            pl.BlockSpec((d, tf), lambda i, f: (0, nf + _serpentine(i, f, nf))),
            pl.BlockSpec((tf, d), lambda i, f: (_serpentine(i, f, nf), 0)),
            pl.BlockSpec((1, d), lambda i, f: (0, 0)),
        ],
        out_specs=pl.BlockSpec((tm, d), lambda i, f: (i, 0)),
        out_shape=jax.ShapeDtypeStruct((t, d), F32),
        scratch_shapes=[pltpu.VMEM((tm, d), BF16), pltpu.VMEM((tm, d), F32)],
        compiler_params=_params(("parallel", "arbitrary"), est),
        name="ffn",
    )(h, norm_w.reshape(1, d), w_in, w_in, w_out, final_w.reshape(1, d))


def _layer_weights(w_in, conv_w, conv_b, dt_bias, a_log, d_skip, ssm_norm_w, d_model):
    d_inner = ssm_norm_w.shape[0]
    n_heads = d_skip.shape[0]
    gn = SSM_GROUPS * STATE_DIM
    conv_dim = d_inner + 2 * gn
    q_dim = d_model
    kv_dim = KV_HEADS * ATTN_HEAD_DIM
    cuts = np.cumsum([d_inner, conv_dim, 2 * n_heads, q_dim, kv_dim, kv_dim, 2 * d_model])
    z_w, xbc_w, dt_w, q_w, k_w, v_w, gate_w = jnp.split(w_in, cuts[:-1], axis=1)
    w_main = jnp.concatenate([z_w, q_w, xbc_w, gate_w, k_w, v_w], axis=1).astype(BF16)
    blocks = {}
    off = 0
    for name, width in (("z", d_inner), ("q", q_dim), ("xbc", conv_dim), ("gate", 2 * d_model), ("k", kv_dim), ("v", kv_dim)):
        blocks[name] = off // COL_BLOCK
        off += width
    n_conv = conv_dim // COL_BLOCK
    conv_w3 = conv_w.reshape(CONV_WIDTH, n_conv, COL_BLOCK).transpose(1, 0, 2)
    conv_b3 = conv_b.reshape(n_conv, 1, COL_BLOCK)
    dtb = dt_bias.reshape(1, 2 * n_heads)
    a_neg = (-jnp.exp(a_log.astype(F32))).reshape(1, 2 * n_heads)
    dskip = jnp.repeat(d_skip.astype(F32), HEAD_DIM_SSM).reshape(SSM_GROUPS, 1, COL_BLOCK)
    nw = ssm_norm_w.astype(F32).reshape(SSM_GROUPS, 1, COL_BLOCK)
    return w_main, dt_w.astype(BF16), blocks, conv_w3, conv_b3, dtb, a_neg, dskip, nw


def _trunk(x, mix_norm_w, w_in, conv_w, conv_b, dt_bias, a_log, d_skip, ssm_norm_w, w_ssm_branch, rel_bias,
           attn_sink, w_attn_branch, w_out, ffn_norm_w, w_ffn_in, w_ffn_out, final_norm_w):
    bsz, seq, d_model = x.shape
    depth = w_in.shape[0]
    h = x.reshape(bsz * seq, d_model)
    bias = _attention_bias(rel_bias).reshape(KV_HEADS, Q_PER_KV * ATTN_BLOCK, 3 * ATTN_BLOCK)
    for l in range(depth):
        w_main, w_dt, blocks, conv_w3, conv_b3, dtb, a_neg, dskip, nw = _layer_weights(
            w_in[l], conv_w[l], conv_b[l], dt_bias[l], a_log[l], d_skip[l], ssm_norm_w[l], d_model)
        proj, dt_raw = _in_proj(h, mix_norm_w[l], w_main, w_dt)
        n_xs = ssm_norm_w.shape[1] // COL_BLOCK
        xs, bm, cm = _conv_silu(proj, conv_w3, conv_b3, bsz, seq, blocks["xbc"], n_xs)
        y_fwd = _ssd_forward(xs, bm, cm, dt_raw, dtb, a_neg, dskip, bsz, seq)
        y_ssm = _ssd_backward(xs, bm, cm, dt_raw, dtb, a_neg, y_fwd, proj, nw, bsz, seq, blocks["z"])
        attn = _attention(proj, bias, attn_sink[l], bsz, seq, blocks["q"], blocks["k"], blocks["v"])
        merged = _merge(y_ssm, attn, proj, w_ssm_branch[l].astype(BF16), w_attn_branch[l].astype(BF16), blocks["gate"])
        h = _out_proj(merged, w_out[l].astype(BF16), h)
        h = _ffn(h, ffn_norm_w[l], w_ffn_in[l].astype(BF16), w_ffn_out[l].astype(BF16), final_norm_w,
                 final_norm=l == depth - 1)
    return h.reshape(bsz, seq, d_model)


def kernel(x_prompt, x_sample, mix_norm_w, w_in, conv_w, conv_b, dt_bias, a_log, d_skip, ssm_norm_w, w_ssm_branch,
           rel_bias, attn_sink, w_attn_branch, w_out, ffn_norm_w, w_ffn_in, w_ffn_out, final_norm_w):
    args = (mix_norm_w, w_in, conv_w, conv_b, dt_bias, a_log, d_skip, ssm_norm_w, w_ssm_branch, rel_bias, attn_sink,
            w_attn_branch, w_out, ffn_norm_w, w_ffn_in, w_ffn_out, final_norm_w)
    return (_trunk(x_prompt, *args), _trunk(x_sample, *args))
```
